```python
import jax, jax.numpy as jnp
from jax import lax
import numpy as np

D_MODEL = 4096
BATCH = 1
SEQ = 8192
DEPTH = 4

CHUNK = 64
N_A = DEPTH // 2
N_B = DEPTH - N_A
GLA_HEADS = 4
GLA_KEY_WIDTH = D_MODEL // 2
GLA_VAL_WIDTH = D_MODEL
GLA_DK = GLA_KEY_WIDTH // GLA_HEADS
GLA_DV = GLA_VAL_WIDTH // GLA_HEADS
GLA_GATE_RANK = 16
GLA_TAU = 16.0
GLA_IN_WIDTH = 2 * GLA_KEY_WIDTH + 2 * GLA_VAL_WIDTH + GLA_GATE_RANK
SB_HEADS = 32
SB_HEAD_DIM = D_MODEL // SB_HEADS
Q_BLOCK = 128
D_FF = ((8 * D_MODEL // 3 + 255) // 256) * 256
CONV_WIDTH = 3
N_MOD = 6
EPS = 1e-6

kernel_name = "yoco_gla_stickbreaking_convffn_adaln"


def rms_norm(x):
    xf = x.astype(jnp.float32)
    return (xf * lax.rsqrt(jnp.mean(xf * xf, axis=-1, keepdims=True) + EPS)).astype(x.dtype)


def modulate(x, shift, scale):
    return rms_norm(x) * (1.0 + scale) + shift


def gla_mixer(h, w_in, w_gate_up, b_gate, norm_g, w_out):
    B, S, _ = h.shape
    N = S // CHUNK
    proj = h @ w_in
    q, k, v, r, g_low = jnp.split(
        proj,
        [GLA_KEY_WIDTH, 2 * GLA_KEY_WIDTH, 2 * GLA_KEY_WIDTH + GLA_VAL_WIDTH,
         2 * GLA_KEY_WIDTH + 2 * GLA_VAL_WIDTH],
        axis=-1)
    log_alpha = jax.nn.log_sigmoid((g_low @ w_gate_up + b_gate).astype(jnp.float32)) / GLA_TAU
    shp_k = (B, N, CHUNK, GLA_HEADS, GLA_DK)
    q = q.astype(jnp.float32).reshape(shp_k) * (GLA_DK ** -0.5)
    k = k.astype(jnp.float32).reshape(shp_k)
    v = v.astype(jnp.float32).reshape(B, N, CHUNK, GLA_HEADS, GLA_DV)
    b_cum = jnp.cumsum(log_alpha.reshape(shp_k), axis=2)
    q_dec = q * jnp.exp(b_cum)
    k_dec = k * jnp.exp(-b_cum)
    causal = jnp.tril(jnp.ones((CHUNK, CHUNK), dtype=bool))
    scores = jnp.einsum('bnthd,bnshd->bnhts', q_dec, k_dec)
    scores = jnp.where(causal, scores, 0.0)
    o_intra = jnp.einsum('bnhts,bnshv->bnthv', scores, v)
    b_last = b_cum[:, :, -1]
    k_state = k * jnp.exp(b_last[:, :, None] - b_cum)
    decay = jnp.exp(b_last)

    def step(state, inp):
        q_n, k_n, v_n, d_n = inp
        o_n = jnp.einsum('bthd,bhdv->bthv', q_n, state)
        state = d_n[..., None] * state + jnp.einsum('bshd,bshv->bhdv', k_n, v_n)
        return state, o_n

    state0 = jnp.zeros((B, GLA_HEADS, GLA_DK, GLA_DV), jnp.float32)
    xs = (q_dec.swapaxes(0, 1), k_state.swapaxes(0, 1), v.swapaxes(0, 1), decay.swapaxes(0, 1))
    _, o_inter = lax.scan(step, state0, xs)
    o = o_intra + o_inter.swapaxes(0, 1)
    o = rms_norm(o) * norm_g
    o = o.reshape(B, S, GLA_VAL_WIDTH).astype(h.dtype)
    return (jax.nn.silu(r) * o) @ w_out


def shared_kv(x, kv_norm_g, w_kv, k_norm_g):
    B, S, _ = x.shape
    src = rms_norm(x) * kv_norm_g
    k, v = jnp.split(src @ w_kv, 2, axis=-1)
    k = rms_norm(k.reshape(B, S, SB_HEADS, SB_HEAD_DIM)) * k_norm_g
    v = v.reshape(B, S, SB_HEADS, SB_HEAD_DIM)
    return k, v


def stick_breaking_mixer(h, k, v, w_q, q_norm_g, w_out):
    B, S, _ = h.shape
    nb = S // Q_BLOCK
    q = rms_norm((h @ w_q).reshape(B, S, SB_HEADS, SB_HEAD_DIM)) * q_norm_g
    q_blocks = q.reshape(B, nb, Q_BLOCK, SB_HEADS, SB_HEAD_DIM).swapaxes(0, 1)
    kf = k.astype(jnp.float32)
    vf = v.astype(jnp.float32)
    key_pos = jnp.arange(S)
    scale = SB_HEAD_DIM ** -0.5

    def block(args):
        q_blk, blk = args
        q_pos = blk * Q_BLOCK + jnp.arange(Q_BLOCK)
        mask = key_pos[None, :] < q_pos[:, None]
        z = jnp.einsum('bqhd,bshd->bhqs', q_blk.astype(jnp.float32), kf) * scale
        log_beta = jax.nn.log_sigmoid(z)
        log_one_minus = jnp.where(mask, jax.nn.log_sigmoid(-z), 0.0)
        log_between = lax.cumsum(log_one_minus, axis=3, reverse=True) - log_one_minus
        att = jnp.where(mask, jnp.exp(log_beta + log_between), 0.0)
        return jnp.einsum('bhqs,bshd->bqhd', att, vf)

    o = lax.map(block, (q_blocks, jnp.arange(nb)))
    o = o.swapaxes(0, 1).reshape(B, S, D_MODEL).astype(h.dtype)
    return o @ w_out


def conv_ffn(h, w_in, conv_w, conv_b, w_out):
    S = h.shape[1]
    u = h @ w_in
    u_pad = jnp.pad(u, ((0, 0), (CONV_WIDTH - 1, 0), (0, 0)))
    acc = conv_b
    for i in range(CONV_WIDTH):
        acc = acc + u_pad[:, i:i + S] * conv_w[i]
    a, b = jnp.split(acc, 2, axis=-1)
    return (jax.nn.silu(a) * b) @ w_out


def setup_inputs(seed: int = 0) -> dict:
    key = jax.random.key(seed)
    ks = jax.random.split(key, 24)
    f32 = jnp.float32
    D = D_MODEL

    def nrm(k, shape, s):
        return jax.random.normal(k, shape, f32) * s

    ada_table = nrm(ks[4], (DEPTH, N_MOD, D), 0.02).at[:, jnp.array([2, 5])].add(1.0)
    return {
        "x": nrm(ks[0], (BATCH, SEQ, D), 1.0),
        "c": nrm(ks[1], (BATCH, D), 1.0),
        "w_ada": nrm(ks[2], (D, N_MOD * D), 0.2 * D ** -0.5),
        "b_ada": nrm(ks[3], (N_MOD * D,), 0.02),
        "ada_table": ada_table,
        "gla_w_in": nrm(ks[5], (N_A, D, GLA_IN_WIDTH), D ** -0.5),
        "gla_w_gate_up": nrm(ks[6], (N_A, GLA_GATE_RANK, GLA_KEY_WIDTH), GLA_GATE_RANK ** -0.5),
        "gla_b_gate": nrm(ks[7], (N_A, GLA_KEY_WIDTH), 0.1),
        "gla_norm_g": 1.0 + nrm(ks[8], (N_A, GLA_DV), 0.02),
        "gla_w_out": nrm(ks[9], (N_A, GLA_VAL_WIDTH, D), GLA_VAL_WIDTH ** -0.5),
        "kv_norm_g": 1.0 + nrm(ks[10], (D,), 0.02),
        "w_kv": nrm(ks[11], (D, 2 * D), D ** -0.5),
        "k_norm_g": 1.0 + nrm(ks[12], (SB_HEAD_DIM,), 0.02),
        "sb_w_q": nrm(ks[13], (N_B, D, D), D ** -0.5),
        "sb_q_norm_g": 1.0 + nrm(ks[14], (N_B, SB_HEAD_DIM), 0.02),
        "sb_w_out": nrm(ks[15], (N_B, D, D), D ** -0.5),
        "ffn_w_in": nrm(ks[16], (DEPTH, D, 2 * D_FF), D ** -0.5),
        "ffn_conv_w": nrm(ks[17], (DEPTH, CONV_WIDTH, 2 * D_FF), CONV_WIDTH ** -0.5),
        "ffn_conv_b": nrm(ks[18], (DEPTH, 2 * D_FF), 0.02),
        "ffn_w_out": nrm(ks[19], (DEPTH, D_FF, D), D_FF ** -0.5),
    }


def reference(x, c, w_ada, b_ada, ada_table, gla_w_in, gla_w_gate_up, gla_b_gate, gla_norm_g,
              gla_w_out, kv_norm_g, w_kv, k_norm_g, sb_w_q, sb_q_norm_g, sb_w_out,
              ffn_w_in, ffn_conv_w, ffn_conv_b, ffn_w_out):
    B = x.shape[0]
    cond = (jax.nn.silu(c) @ w_ada + b_ada).reshape(B, N_MOD, D_MODEL)
    k_sh = v_sh = None
    for l in range(DEPTH):
        mod = cond + ada_table[l]
        shift_m, scale_m, gate_m, shift_f, scale_f, gate_f = [mod[:, i, None, :] for i in range(N_MOD)]
        h = modulate(x, shift_m, scale_m)
        if l < N_A:
            y = gla_mixer(h, gla_w_in[l], gla_w_gate_up[l], gla_b_gate[l], gla_norm_g[l], gla_w_out[l])
        else:
            if l == N_A:
                k_sh, v_sh = shared_kv(x, kv_norm_g, w_kv, k_norm_g)
            j = l - N_A
            y = stick_breaking_mixer(h, k_sh, v_sh, sb_w_q[j], sb_q_norm_g[j], sb_w_out[j])
        x = x + gate_m * y
        h = modulate(x, shift_f, scale_f)
        x = x + gate_f * conv_ffn(h, ffn_w_in[l], ffn_conv_w[l], ffn_conv_b[l], ffn_w_out[l])
    return x
```

```python
import functools

import jax
import jax.numpy as jnp
from jax import lax
from jax.experimental import pallas as pl
from jax.experimental.pallas import tpu as pltpu

F32 = jnp.float32
BF16 = jnp.bfloat16

EPS = 1e-6
N_MOD = 6
CONV_WIDTH = 3
GLA_CHUNK = 64
GLA_HEADS = 4
GLA_TAU = 16.0
SB_HEAD_DIM = 128

V7X_LANES = 128
V7X_SUBLANES = 8
V7X_VMEM_LIMIT_BYTES = 56 * 1024 * 1024

F32_EXP_UNDERFLOW = -110.0


def _params(*semantics):
    return pltpu.CompilerParams(dimension_semantics=semantics,
                                vmem_limit_bytes=V7X_VMEM_LIMIT_BYTES)


def _tile(n, pref):
    if n <= pref:
        return n
    t = (pref // V7X_LANES) * V7X_LANES
    while t >= V7X_LANES:
        if n % t == 0:
            return t
        t -= V7X_LANES
    return n


def _dot(a, b):
    return jnp.dot(a, b, preferred_element_type=F32)


def _dot_nt(a, b):
    return lax.dot_general(a, b, (((1,), (1,)), ((), ())), preferred_element_type=F32)


def _dot_tn(a, b):
    return lax.dot_general(a, b, (((0,), (0,)), ((), ())), preferred_element_type=F32)


def _silu(x):
    return x / (1.0 + jnp.exp(-x))


def _split_bf16(x):
    hi = x.astype(BF16)
    lo = (x - hi.astype(F32)).astype(BF16)
    return hi, lo


def _cond_kernel(c_ref, w_ref, b_ref, tab_ref, o_ref):
    s = _silu(c_ref[...]).astype(BF16)
    acc = _dot(s, w_ref[...].astype(BF16))
    o_ref[...] = acc[0:1, :] + b_ref[...] + tab_ref[...]


def _cond(c, w_ada, b_ada, ada_table):
    d = c.shape[1]
    depth = ada_table.shape[0]
    n = w_ada.shape[1]
    tn = _tile(n, 512)
    c8 = jnp.broadcast_to(c, (V7X_SUBLANES, d))
    return pl.pallas_call(
        _cond_kernel,
        grid=(n // tn,),
        in_specs=[pl.BlockSpec((V7X_SUBLANES, d), lambda j: (0, 0)),
                  pl.BlockSpec((d, tn), lambda j: (0, j)),
                  pl.BlockSpec((1, tn), lambda j: (0, j)),
                  pl.BlockSpec((depth, tn), lambda j: (0, j))],
        out_specs=pl.BlockSpec((depth, tn), lambda j: (0, j)),
        out_shape=jax.ShapeDtypeStruct((depth, n), F32),
        compiler_params=_params("arbitrary"),
        name="cond",
    )(c8, w_ada, b_ada.reshape(1, n), ada_table.reshape(depth, n))


def _norm_affine_kernel(x_ref, mul_ref, add_ref, o_ref, *, plus_one):
    x = x_ref[...]
    ms = jnp.mean(x * x, axis=-1, keepdims=True)
    xn = x * lax.rsqrt(ms + EPS)
    mul = mul_ref[...]
    if plus_one:
        mul = 1.0 + mul
    o_ref[...] = (xn * mul + add_ref[...]).astype(o_ref.dtype)


def _norm_affine(x, mul, add, plus_one):
    s, d = x.shape
    tm = _tile(s, 256)
    return pl.pallas_call(
        functools.partial(_norm_affine_kernel, plus_one=plus_one),
        grid=(s // tm,),
        in_specs=[pl.BlockSpec((tm, d), lambda i: (i, 0)),
                  pl.BlockSpec((1, d), lambda i: (0, 0)),
                  pl.BlockSpec((1, d), lambda i: (0, 0))],
        out_specs=pl.BlockSpec((tm, d), lambda i: (i, 0)),
        out_shape=jax.ShapeDtypeStruct((s, d), BF16),
        compiler_params=_params("arbitrary"),
        name="norm_affine",
    )(x, mul.reshape(1, d), add.reshape(1, d))


def _mm_plain_kernel(a_ref, w_ref, o_ref):
    o_ref[...] = _dot(a_ref[...], w_ref[...]).astype(o_ref.dtype)


def _mm_headnorm_kernel(a_ref, w_ref, g_ref, o_ref, *, post_scale):
    acc = _dot(a_ref[...], w_ref[...])
    g = g_ref[...] * post_scale
    for c in range(acc.shape[1] // SB_HEAD_DIM):
        sl = slice(c * SB_HEAD_DIM, (c + 1) * SB_HEAD_DIM)
        blk = acc[:, sl]
        ms = jnp.mean(blk * blk, axis=-1, keepdims=True)
        o_ref[:, sl] = (blk * lax.rsqrt(ms + EPS) * g).astype(o_ref.dtype)


def _mm_residual_kernel(a_ref, w_ref, x_ref, gate_ref, o_ref):
    o_ref[...] = x_ref[...] + gate_ref[...] * _dot(a_ref[...], w_ref[...])


def _mm_tiles(m, n, k):
    tm = _tile(m, 1024)
    tn = _tile(n, 1024 if k <= 4096 else 512)
    if k > 4096:
        tm = _tile(m, 512)
    return tm, tn


def _mm_plain(a, w, out_dtype):
    m, k = a.shape
    n = w.shape[1]
    tm, tn = _mm_tiles(m, n, k)
    return pl.pallas_call(
        _mm_plain_kernel,
        grid=(m // tm, n // tn),
        in_specs=[pl.BlockSpec((tm, k), lambda i, j: (i, 0)),
                  pl.BlockSpec((k, tn), lambda i, j: (0, j))],
        out_specs=pl.BlockSpec((tm, tn), lambda i, j: (i, j)),
        out_shape=jax.ShapeDtypeStruct((m, n), out_dtype),
        compiler_params=_params("arbitrary", "arbitrary"),
        name="mm_plain",
    )(a, w)


def _mm_headnorm(a, w, g, post_scale):
    m, k = a.shape
    n = w.shape[1]
    tm, tn = _mm_tiles(m, n, k)
    return pl.pallas_call(
        functools.partial(_mm_headnorm_kernel, post_scale=post_scale),
        grid=(m // tm, n // tn),
        in_specs=[pl.BlockSpec((tm, k), lambda i, j: (i, 0)),
                  pl.BlockSpec((k, tn), lambda i, j: (0, j)),
                  pl.BlockSpec((1, SB_HEAD_DIM), lambda i, j: (0, 0))],
        out_specs=pl.BlockSpec((tm, tn), lambda i, j: (i, j)),
        out_shape=jax.ShapeDtypeStruct((m, n), BF16),
        compiler_params=_params("arbitrary", "arbitrary"),
        name="mm_headnorm",
    )(a, w, g.reshape(1, SB_HEAD_DIM))


def _mm_residual(a, w, x, gate):
    m, k = a.shape
    n = w.shape[1]
    tm, tn = _mm_tiles(m, n, k)
    return pl.pallas_call(
        _mm_residual_kernel,
        grid=(m // tm, n // tn),
        in_specs=[pl.BlockSpec((tm, k), lambda i, j: (i, 0)),
                  pl.BlockSpec((k, tn), lambda i, j: (0, j)),
                  pl.BlockSpec((tm, tn), lambda i, j: (i, j)),
                  pl.BlockSpec((1, tn), lambda i, j: (0, j))],
        out_specs=pl.BlockSpec((tm, tn), lambda i, j: (i, j)),
        out_shape=jax.ShapeDtypeStruct((m, n), F32),
        compiler_params=_params("arbitrary", "arbitrary"),
        name="mm_residual",
    )(a, w, x, gate.reshape(1, n))


def _ffn_in_kernel(h_ref, wa_ref, wb_ref, cwa_ref, cwb_ref, cba_ref, cbb_ref,
                   o_ref, carry_a_ref, carry_b_ref):
    tm = h_ref.shape[0]

    @pl.when(pl.program_id(1) == 0)
    def _():
        carry_a_ref[...] = jnp.zeros_like(carry_a_ref)
        carry_b_ref[...] = jnp.zeros_like(carry_b_ref)

    h = h_ref[...]
    row = lax.broadcasted_iota(jnp.int32, carry_a_ref.shape, 0)

    def conv(u, carry_ref, cw_ref, cb_ref):
        w0, w1, w2 = cw_ref[0:1, :], cw_ref[1:2, :], cw_ref[2:3, :]
        bias = cb_ref[...]
        u1 = pltpu.roll(u, 1, axis=0)
        u2 = pltpu.roll(u, 2, axis=0)
        prev = carry_ref[...]
        head1 = jnp.where(row < 1, pltpu.roll(prev, 1, axis=0), u1[0:V7X_SUBLANES])
        head2 = jnp.where(row < 2, pltpu.roll(prev, 2, axis=0), u2[0:V7X_SUBLANES])
        head = bias + head2 * w0 + head1 * w1 + u[0:V7X_SUBLANES] * w2
        tail = (bias + u2[V7X_SUBLANES:] * w0 + u1[V7X_SUBLANES:] * w1
                + u[V7X_SUBLANES:] * w2)
        carry_ref[...] = u[tm - V7X_SUBLANES:tm]
        return head, tail

    a_head, a_tail = conv(_dot(h, wa_ref[...]), carry_a_ref, cwa_ref, cba_ref)
    b_head, b_tail = conv(_dot(h, wb_ref[...]), carry_b_ref, cwb_ref, cbb_ref)
    o_ref[0:V7X_SUBLANES, :] = (_silu(a_head) * b_head).astype(o_ref.dtype)
    o_ref[V7X_SUBLANES:, :] = (_silu(a_tail) * b_tail).astype(o_ref.dtype)


def _ffn_in(h, w_in, conv_w, conv_b):
    s, d = h.shape
    f = w_in.shape[1] // 2
    tm = _tile(s, 1024)
    tn = _tile(f, 256)
    nj = f // tn
    return pl.pallas_call(
        _ffn_in_kernel,
        grid=(nj, s // tm),
        in_specs=[pl.BlockSpec((tm, d), lambda j, i: (i, 0)),
                  pl.BlockSpec((d, tn), lambda j, i: (0, j)),
                  pl.BlockSpec((d, tn), lambda j, i: (0, j + nj)),
                  pl.BlockSpec((CONV_WIDTH, tn), lambda j, i: (0, j)),
                  pl.BlockSpec((CONV_WIDTH, tn), lambda j, i: (0, j + nj)),
                  pl.BlockSpec((1, tn), lambda j, i: (0, j)),
                  pl.BlockSpec((1, tn), lambda j, i: (0, j + nj))],
        out_specs=pl.BlockSpec((tm, tn), lambda j, i: (i, j)),
        out_shape=jax.ShapeDtypeStruct((s, f), BF16),
        scratch_shapes=[pltpu.VMEM((V7X_SUBLANES, tn), F32),
                        pltpu.VMEM((V7X_SUBLANES, tn), F32)],
        compiler_params=_params("arbitrary", "arbitrary"),
        name="ffn_in",
    )(h, w_in, w_in, conv_w, conv_w, conv_b.reshape(1, 2 * f), conv_b.reshape(1, 2 * f))


def _gla_kernel(qk_ref, v_ref, r_ref, gl_ref, wgu_ref, bg_ref, ng_ref, tril_ref,
                o_ref, state_ref, *, chunks, dk, dv):
    kw = GLA_HEADS * dk

    @pl.when(pl.program_id(0) == 0)
    def _():
        state_ref[...] = jnp.zeros_like(state_ref)

    tril = tril_ref[...]
    causal = tril > 0
    ones = jnp.ones((GLA_CHUNK, V7X_LANES), BF16)
    q_scale = dk ** -0.5

    def chunk(c, carry):
        rows = pl.ds(pl.multiple_of(c * GLA_CHUNK, GLA_CHUNK), GLA_CHUNK)
        y = _dot(gl_ref[rows, :].astype(BF16), wgu_ref[...]) + bg_ref[...]
        log_alpha = (jnp.minimum(y, 0.0) - jnp.log(1.0 + jnp.exp(-jnp.abs(y)))) * (1.0 / GLA_TAU)
        hi, lo = _split_bf16(log_alpha)
        b_cum = _dot(tril, hi) + _dot(tril, lo)
        b_last_col = _dot_tn(hi, ones) + _dot_tn(lo, ones)
        b_last_row = b_cum[GLA_CHUNK - 1:GLA_CHUNK, :]
        e_pos = jnp.exp(b_cum)
        e_neg = jnp.exp(-b_cum)
        e_end = jnp.exp(b_last_row - b_cum)
        for h in range(GLA_HEADS):
            ksl = slice(h * dk, (h + 1) * dk)
            vsl = slice(h * dv, (h + 1) * dv)
            q = qk_ref[rows, ksl]
            k = qk_ref[rows, kw + h * dk:kw + (h + 1) * dk]
            q_dec = (q * q_scale * e_pos[:, ksl]).astype(BF16)
            k_dec = (k * e_neg[:, ksl]).astype(BF16)
            k_end = (k * e_end[:, ksl]).astype(BF16)
            v = v_ref[rows, vsl]
            scores = jnp.where(causal, _dot_nt(q_dec, k_dec), 0.0).astype(BF16)
            state = state_ref[h]
            o = _dot(scores, v) + _dot(q_dec, state.astype(BF16))
            decay = jnp.exp(b_last_col[ksl, :])
            decay = jnp.concatenate([decay] * (dv // V7X_LANES), axis=1)
            state_ref[h] = state * decay + _dot_tn(k_end, v)
            ms = jnp.mean(o * o, axis=-1, keepdims=True)
            o = o * lax.rsqrt(ms + EPS) * ng_ref[...]
            o_ref[rows, vsl] = (_silu(r_ref[rows, vsl]) * o).astype(o_ref.dtype)
        return carry

    lax.fori_loop(0, chunks, chunk, 0)


def _gla(qk, v, r, g_low, w_gate_up, b_gate, norm_g):
    s = qk.shape[0]
    kw = w_gate_up.shape[1]
    vw = v.shape[1]
    dk, dv = kw // GLA_HEADS, vw // GLA_HEADS
    rows = _tile(s, 256)
    chunks = rows // GLA_CHUNK
    tril = jnp.tril(jnp.ones((GLA_CHUNK, GLA_CHUNK), BF16))
    return pl.pallas_call(
        functools.partial(_gla_kernel, chunks=chunks, dk=dk, dv=dv),
        grid=(s // rows,),
        in_specs=[pl.BlockSpec((rows, 2 * kw), lambda i: (i, 0)),
                  pl.BlockSpec((rows, vw), lambda i: (i, 0)),
                  pl.BlockSpec((rows, vw), lambda i: (i, 0)),
                  pl.BlockSpec((rows, V7X_LANES), lambda i: (i, 0)),
                  pl.BlockSpec((V7X_LANES, kw), lambda i: (0, 0)),
                  pl.BlockSpec((1, kw), lambda i: (0, 0)),
                  pl.BlockSpec((1, dv), lambda i: (0, 0)),
                  pl.BlockSpec((GLA_CHUNK, GLA_CHUNK), lambda i: (0, 0))],
        out_specs=pl.BlockSpec((rows, vw), lambda i: (i, 0)),
        out_shape=jax.ShapeDtypeStruct((s, vw), BF16),
        scratch_shapes=[pltpu.VMEM((GLA_HEADS, dk, dv), F32)],
        compiler_params=_params("arbitrary"),
        name="gla",
    )(qk, v, r, g_low, w_gate_up, b_gate.reshape(1, kw), norm_g.reshape(1, dv), tril)


def _sb_kernel(q_ref, k_ref, v_ref, t_ref, o_ref, acc_ref, run_ref, *, tq, tk):
    qi = pl.program_id(1)
    q = q_ref[...]
    tmat = t_ref[...]
    diag_blocks = tq // tk
    q_pos = qi * tq + lax.broadcasted_iota(jnp.int32, (tq, tk), 0)
    k_off = lax.broadcasted_iota(jnp.int32, (tq, tk), 1)

    acc_ref[...] = jnp.zeros_like(acc_ref)
    run_ref[...] = jnp.zeros_like(run_ref)

    def block(j, masked):
        start = pl.multiple_of(j * tk, tk)
        kj = k_ref[pl.ds(start, tk), :]
        vj = v_ref[pl.ds(start, tk), :]
        z = _dot_nt(q, kj)
        softplus = jnp.maximum(z, 0.0) + jnp.log(1.0 + jnp.exp(-jnp.abs(z)))
        log_beta = z - softplus
        neg_log_rest = softplus
        if masked:
            mask = (start + k_off) < q_pos
            neg_log_rest = jnp.where(mask, neg_log_rest, 0.0)
        hi, lo = _split_bf16(neg_log_rest)
        cm = _dot(jnp.concatenate([hi, lo], axis=1), tmat)
        att = jnp.exp(log_beta + cm[:, :tk] + run_ref[...])
        if masked:
            att = jnp.where(mask, att, 0.0)
        acc_ref[...] += _dot(att.astype(BF16), vj)
        run = run_ref[...] + cm[:, tk:]
        run_ref[...] = run
        return jnp.max(run)

    top = jnp.float32(0.0)
    for d in range(diag_blocks):
        top = block(qi * diag_blocks + (diag_blocks - 1 - d), True)

    def cond(st):
        j, top = st
        return jnp.logical_and(j >= 0, top > F32_EXP_UNDERFLOW)

    def body(st):
        j, _ = st
        return j - 1, block(j, False)

    lax.while_loop(cond, body, (qi * diag_blocks - 1, top))
    o_ref[...] = acc_ref[...].astype(o_ref.dtype)


def _sb_attention(q, k, v):
    s, d = q.shape
    dh = SB_HEAD_DIM
    tk = V7X_LANES
    tq = _tile(s, 256)
    r = lax.broadcasted_iota(jnp.int32, (tk, 2 * tk), 0)
    c = lax.broadcasted_iota(jnp.int32, (tk, 2 * tk), 1)
    half = jnp.where(jnp.logical_or(c >= tk, r > c), -1.0, 0.0).astype(BF16)
    tmat = jnp.concatenate([half, half], axis=0)
    return pl.pallas_call(
        functools.partial(_sb_kernel, tq=tq, tk=tk),
        grid=(d // dh, s // tq),
        in_specs=[pl.BlockSpec((tq, dh), lambda h, i: (i, h)),
                  pl.BlockSpec((s, dh), lambda h, i: (0, h)),
                  pl.BlockSpec((s, dh), lambda h, i: (0, h)),
                  pl.BlockSpec((2 * tk, 2 * tk), lambda h, i: (0, 0))],
        out_specs=pl.BlockSpec((tq, dh), lambda h, i: (i, h)),
        out_shape=jax.ShapeDtypeStruct((s, d), BF16),
        scratch_shapes=[pltpu.VMEM((tq, dh), F32), pltpu.VMEM((tq, tk), F32)],
        compiler_params=_params("arbitrary", "arbitrary"),
        name="sb_attention",
    )(q, k, v, tmat)


def _ffn_block(x, mod, l, ffn_w_in, ffn_conv_w, ffn_conv_b, ffn_w_out):
    d = x.shape[1]
    shift, scale, gate = (mod[i * d:(i + 1) * d] for i in (3, 4, 5))
    h = _norm_affine(x, scale, shift, True)
    g = _ffn_in(h, ffn_w_in[l].astype(BF16), ffn_conv_w[l], ffn_conv_b[l])
    return _mm_residual(g, ffn_w_out[l].astype(BF16), x, gate)


def kernel(x, c, w_ada, b_ada, ada_table, gla_w_in, gla_w_gate_up, gla_b_gate, gla_norm_g, gla_w_out, kv_norm_g, w_kv, k_norm_g, sb_w_q, sb_q_norm_g, sb_w_out, ffn_w_in, ffn_conv_w, ffn_conv_b, ffn_w_out):
    batch, s, d = x.shape
    assert batch == 1, "adaLN modulation rows are built for a single sequence"
    depth = ada_table.shape[0]
    n_a = gla_w_in.shape[0]
    kw = gla_w_gate_up.shape[2]
    rank = gla_w_gate_up.shape[1]
    x = x.reshape(s, d)

    mods = _cond(c, w_ada, b_ada, ada_table)
    k_sh = v_sh = None
    for l in range(depth):
        mod = mods[l]
        shift, scale, gate = (mod[i * d:(i + 1) * d] for i in (0, 1, 2))
        h = _norm_affine(x, scale, shift, True)
        if l < n_a:
            w_in = gla_w_in[l]
            qk = _mm_plain(h, w_in[:, :2 * kw].astype(BF16), F32)
            v = _mm_plain(h, w_in[:, 2 * kw:2 * kw + d].astype(BF16), BF16)
            r = _mm_plain(h, w_in[:, 2 * kw + d:2 * kw + 2 * d].astype(BF16), F32)
            w_low = jnp.pad(w_in[:, 2 * kw + 2 * d:], ((0, 0), (0, V7X_LANES - rank)))
            g_low = _mm_plain(h, w_low.astype(BF16), F32)
            w_up = jnp.pad(gla_w_gate_up[l], ((0, V7X_LANES - rank), (0, 0))).astype(BF16)
            o = _gla(qk, v, r, g_low, w_up, gla_b_gate[l], gla_norm_g[l])
            x = _mm_residual(o, gla_w_out[l].astype(BF16), x, gate)
        else:
            if l == n_a:
                src = _norm_affine(x, kv_norm_g, jnp.zeros_like(kv_norm_g), False)
                k_sh = _mm_headnorm(src, w_kv[:, :d].astype(BF16), k_norm_g, 1.0)
                v_sh = _mm_plain(src, w_kv[:, d:].astype(BF16), BF16)
            j = l - n_a
            q = _mm_headnorm(h, sb_w_q[j].astype(BF16), sb_q_norm_g[j], SB_HEAD_DIM ** -0.5)
            o = _sb_attention(q, k_sh, v_sh)
            x = _mm_residual(o, sb_w_out[j].astype(BF16), x, gate)
        x = _ffn_block(x, mod, l, ffn_w_in, ffn_conv_w, ffn_conv_b, ffn_w_out)
    return x.reshape(batch, s, d)
```

```python
import functools

import jax
import jax.numpy as jnp
from jax import lax
from jax.experimental import pallas as pl
from jax.experimental.pallas import tpu as pltpu

F32 = jnp.float32
BF16 = jnp.bfloat16

EPS = 1e-6
N_MOD = 6
CONV_WIDTH = 3
GLA_CHUNK = 64
GLA_HEADS = 4
GLA_TAU = 16.0
SB_HEAD_DIM = 128

V7X_LANES = 128
V7X_SUBLANES = 8
V7X_VMEM_LIMIT_BYTES = 56 * 1024 * 1024

F32_EXP_UNDERFLOW = -104.0


def _params(*semantics):
    return pltpu.CompilerParams(dimension_semantics=semantics,
                                vmem_limit_bytes=V7X_VMEM_LIMIT_BYTES)


def _tile(n, pref):
    if n <= pref:
        return n
    t = (pref // V7X_LANES) * V7X_LANES
    while t >= V7X_LANES:
        if n % t == 0:
            return t
        t -= V7X_LANES
    return n


def _dot(a, b):
    return jnp.dot(a, b, preferred_element_type=F32)


def _dot_nt(a, b):
    return lax.dot_general(a, b, (((1,), (1,)), ((), ())), preferred_element_type=F32)


def _dot_tn(a, b):
    return lax.dot_general(a, b, (((0,), (0,)), ((), ())), preferred_element_type=F32)


def _silu(x):
    return x / (1.0 + jnp.exp(-x))


def _split_bf16(x):
    hi = x.astype(BF16)
    lo = (x - hi.astype(F32)).astype(BF16)
    return hi, lo


def _cond_kernel(c_ref, w_ref, b_ref, tab_ref, o_ref):
    s = _silu(c_ref[...]).astype(BF16)
    acc = _dot(s, w_ref[...].astype(BF16))
    o_ref[...] = acc[0:1, :] + b_ref[...] + tab_ref[...]


def _cond(c, w_ada, b_ada, ada_table):
    d = c.shape[1]
    depth = ada_table.shape[0]
    n = w_ada.shape[1]
    tn = _tile(n, 512)
    c8 = jnp.broadcast_to(c, (V7X_SUBLANES, d))
    return pl.pallas_call(
        _cond_kernel,
        grid=(n // tn,),
        in_specs=[pl.BlockSpec((V7X_SUBLANES, d), lambda j: (0, 0)),
                  pl.BlockSpec((d, tn), lambda j: (0, j)),
                  pl.BlockSpec((1, tn), lambda j: (0, j)),
                  pl.BlockSpec((depth, tn), lambda j: (0, j))],
        out_specs=pl.BlockSpec((depth, tn), lambda j: (0, j)),
        out_shape=jax.ShapeDtypeStruct((depth, n), F32),
        compiler_params=_params("arbitrary"),
        name="cond",
    )(c8, w_ada, b_ada.reshape(1, n), ada_table.reshape(depth, n))


def _norm_affine_kernel(x_ref, mul_ref, add_ref, o_ref, *, plus_one):
    x = x_ref[...]
    ms = jnp.mean(x * x, axis=-1, keepdims=True)
    xn = x * lax.rsqrt(ms + EPS)
    mul = mul_ref[...]
    if plus_one:
        mul = 1.0 + mul
    o_ref[...] = (xn * mul + add_ref[...]).astype(o_ref.dtype)


def _norm_affine(x, mul, add, plus_one):
    s, d = x.shape
    tm = _tile(s, 256)
    return pl.pallas_call(
        functools.partial(_norm_affine_kernel, plus_one=plus_one),
        grid=(s // tm,),
        in_specs=[pl.BlockSpec((tm, d), lambda i: (i, 0)),
                  pl.BlockSpec((1, d), lambda i: (0, 0)),
                  pl.BlockSpec((1, d), lambda i: (0, 0))],
        out_specs=pl.BlockSpec((tm, d), lambda i: (i, 0)),
        out_shape=jax.ShapeDtypeStruct((s, d), BF16),
        compiler_params=_params("arbitrary"),
        name="norm_affine",
    )(x, mul.reshape(1, d), add.reshape(1, d))


def _mm_plain_kernel(a_ref, w_ref, o_ref, *, valid_cols=None):
    w = w_ref[...]
    if valid_cols is not None:
        col = lax.broadcasted_iota(jnp.int32, w.shape, 1)
        w = jnp.where(col < valid_cols, w, jnp.zeros_like(w))
    o_ref[...] = _dot(a_ref[...], w).astype(o_ref.dtype)


def _mm_headnorm_kernel(a_ref, w_ref, g_ref, o_ref, *, post_scale):
    acc = _dot(a_ref[...], w_ref[...])
    g = g_ref[...] * post_scale
    for c in range(acc.shape[1] // SB_HEAD_DIM):
        sl = slice(c * SB_HEAD_DIM, (c + 1) * SB_HEAD_DIM)
        blk = acc[:, sl]
        ms = jnp.mean(blk * blk, axis=-1, keepdims=True)
        o_ref[:, sl] = (blk * lax.rsqrt(ms + EPS) * g).astype(o_ref.dtype)


def _mm_residual_kernel(a_ref, w_ref, x_ref, gate_ref, o_ref):
    o_ref[...] = x_ref[...] + gate_ref[...] * _dot(a_ref[...], w_ref[...])


def _mm_tiles(m, n, k):
    tm = _tile(m, 1024)
    tn = _tile(n, 1024 if k <= 4096 else 512)
    if k > 4096:
        tm = _tile(m, 512)
    return tm, tn


def _weight_spec(w, layer, col0, tn):
    k = w.shape[-2]
    assert col0 % tn == 0
    c0 = col0 // tn
    if w.ndim == 2:
        return pl.BlockSpec((k, tn), lambda i, j: (0, c0 + j))
    return pl.BlockSpec((None, k, tn), lambda i, j: (layer, 0, c0 + j))


def _mm_plain(a, w, out_dtype, *, layer=0, col0=0, n=None, valid_cols=None):
    m, k = a.shape
    n = w.shape[-1] if n is None else n
    tm, tn = _mm_tiles(m, n, k)
    return pl.pallas_call(
        functools.partial(_mm_plain_kernel, valid_cols=valid_cols),
        grid=(m // tm, n // tn),
        in_specs=[pl.BlockSpec((tm, k), lambda i, j: (i, 0)),
                  _weight_spec(w, layer, col0, tn)],
        out_specs=pl.BlockSpec((tm, tn), lambda i, j: (i, j)),
        out_shape=jax.ShapeDtypeStruct((m, n), out_dtype),
        compiler_params=_params("arbitrary", "arbitrary"),
        name="mm_plain",
    )(a, w)


def _mm_headnorm(a, w, g, post_scale, *, layer=0, col0=0, n=None):
    m, k = a.shape
    n = w.shape[-1] if n is None else n
    tm, tn = _mm_tiles(m, n, k)
    return pl.pallas_call(
        functools.partial(_mm_headnorm_kernel, post_scale=post_scale),
        grid=(m // tm, n // tn),
        in_specs=[pl.BlockSpec((tm, k), lambda i, j: (i, 0)),
                  _weight_spec(w, layer, col0, tn),
                  pl.BlockSpec((1, SB_HEAD_DIM), lambda i, j: (0, 0))],
        out_specs=pl.BlockSpec((tm, tn), lambda i, j: (i, j)),
        out_shape=jax.ShapeDtypeStruct((m, n), BF16),
        compiler_params=_params("arbitrary", "arbitrary"),
        name="mm_headnorm",
    )(a, w, g.reshape(1, SB_HEAD_DIM))


def _mm_residual(a, w, x, gate, *, layer=0):
    m, k = a.shape
    n = w.shape[-1]
    tm, tn = _mm_tiles(m, n, k)
    return pl.pallas_call(
        _mm_residual_kernel,
        grid=(m // tm, n // tn),
        in_specs=[pl.BlockSpec((tm, k), lambda i, j: (i, 0)),
                  _weight_spec(w, layer, 0, tn),
                  pl.BlockSpec((tm, tn), lambda i, j: (i, j)),
                  pl.BlockSpec((1, tn), lambda i, j: (0, j))],
        out_specs=pl.BlockSpec((tm, tn), lambda i, j: (i, j)),
        out_shape=jax.ShapeDtypeStruct((m, n), F32),
        compiler_params=_params("arbitrary", "arbitrary"),
        name="mm_residual",
    )(a, w, x, gate.reshape(1, n))


FFN_ROW_CHUNKS = 8


def _ffn_in_kernel(h_ref, wa_ref, wb_ref, cwa_ref, cwb_ref, cba_ref, cbb_ref,
                   o_ref, wa_bf_ref, wb_bf_ref, carry_a_ref, carry_b_ref, ua_ref, ub_ref):
    tm = h_ref.shape[0]
    cm = tm // FFN_ROW_CHUNKS

    @pl.when(pl.program_id(1) == 0)
    def _():
        wa_bf_ref[...] = wa_ref[...].astype(BF16)
        wb_bf_ref[...] = wb_ref[...].astype(BF16)
        carry_a_ref[...] = jnp.zeros_like(carry_a_ref)
        carry_b_ref[...] = jnp.zeros_like(carry_b_ref)

    row = lax.broadcasted_iota(jnp.int32, carry_a_ref.shape, 0)

    def conv(u, prev, cw_ref, cb_ref):
        w0, w1, w2 = cw_ref[0:1, :], cw_ref[1:2, :], cw_ref[2:3, :]
        bias = cb_ref[...]
        u1 = pltpu.roll(u, 1, axis=0)
        u2 = pltpu.roll(u, 2, axis=0)
        head1 = jnp.where(row < 1, pltpu.roll(prev, 1, axis=0), u1[0:V7X_SUBLANES])
        head2 = jnp.where(row < 2, pltpu.roll(prev, 2, axis=0), u2[0:V7X_SUBLANES])
        head = bias + head2 * w0 + head1 * w1 + u[0:V7X_SUBLANES] * w2
        tail = (bias + u2[V7X_SUBLANES:] * w0 + u1[V7X_SUBLANES:] * w1
                + u[V7X_SUBLANES:] * w2)
        return head, tail

    prev_a = carry_a_ref[...]
    prev_b = carry_b_ref[...]
    for c in range(FFN_ROW_CHUNKS):
        r0 = c * cm
        h = h_ref[r0:r0 + cm, :]
        slot = lax.rem(pl.program_id(1) + c, 2)
        ua_ref[slot] = _dot(h, wa_bf_ref[...])
        ub_ref[slot] = _dot(h, wb_bf_ref[...])
        ua = ua_ref[slot]
        ub = ub_ref[slot]
        a_head, a_tail = conv(ua, prev_a, cwa_ref, cba_ref)
        b_head, b_tail = conv(ub, prev_b, cwb_ref, cbb_ref)
        o_ref[r0:r0 + V7X_SUBLANES, :] = (_silu(a_head) * b_head).astype(o_ref.dtype)
        o_ref[r0 + V7X_SUBLANES:r0 + cm, :] = (_silu(a_tail) * b_tail).astype(o_ref.dtype)
        prev_a = ua[cm - V7X_SUBLANES:cm]
        prev_b = ub[cm - V7X_SUBLANES:cm]
    carry_a_ref[...] = prev_a
    carry_b_ref[...] = prev_b


def _ffn_in(h, w_in, conv_w, conv_b, layer):
    s, d = h.shape
    f = w_in.shape[-1] // 2
    tm = _tile(s, 1024)
    tn = _tile(f, 256)
    nj = f // tn
    conv_b = conv_b.reshape(conv_b.shape[0], 1, 2 * f)
    return pl.pallas_call(
        _ffn_in_kernel,
        grid=(nj, s // tm),
        in_specs=[pl.BlockSpec((tm, d), lambda j, i: (i, 0)),
                  pl.BlockSpec((None, d, tn), lambda j, i: (layer, 0, j)),
                  pl.BlockSpec((None, d, tn), lambda j, i: (layer, 0, j + nj)),
                  pl.BlockSpec((None, CONV_WIDTH, tn), lambda j, i: (layer, 0, j)),
                  pl.BlockSpec((None, CONV_WIDTH, tn), lambda j, i: (layer, 0, j + nj)),
                  pl.BlockSpec((None, 1, tn), lambda j, i: (layer, 0, j)),
                  pl.BlockSpec((None, 1, tn), lambda j, i: (layer, 0, j + nj))],
        out_specs=pl.BlockSpec((tm, tn), lambda j, i: (i, j)),
        out_shape=jax.ShapeDtypeStruct((s, f), BF16),
        scratch_shapes=[pltpu.VMEM((d, tn), BF16),
                        pltpu.VMEM((d, tn), BF16),
                        pltpu.VMEM((V7X_SUBLANES, tn), F32),
                        pltpu.VMEM((V7X_SUBLANES, tn), F32),
                        pltpu.VMEM((2, tm // FFN_ROW_CHUNKS, tn), F32),
                        pltpu.VMEM((2, tm // FFN_ROW_CHUNKS, tn), F32)],
        compiler_params=_params("arbitrary", "arbitrary"),
        name="ffn_in",
    )(h, w_in, w_in, conv_w, conv_w, conv_b, conv_b)


def _gla_kernel(qk_ref, v_ref, r_ref, gl_ref, wgu_ref, bg_ref, ng_ref, tril_ref,
                o_ref, state_ref, *, chunks, dk, dv):
    kw = GLA_HEADS * dk

    @pl.when(pl.program_id(0) == 0)
    def _():
        state_ref[...] = jnp.zeros_like(state_ref)

    tril = tril_ref[...]
    causal = tril > 0
    ones = jnp.ones((GLA_CHUNK, V7X_LANES), BF16)
    q_scale = dk ** -0.5

    def chunk(c, carry):
        rows = pl.ds(pl.multiple_of(c * GLA_CHUNK, GLA_CHUNK), GLA_CHUNK)
        y = _dot(gl_ref[rows, :].astype(BF16), wgu_ref[...]) + bg_ref[...]
        log_alpha = (jnp.minimum(y, 0.0) - jnp.log(1.0 + jnp.exp(-jnp.abs(y)))) * (1.0 / GLA_TAU)
        hi, lo = _split_bf16(log_alpha)
        b_cum = _dot(tril, hi) + _dot(tril, lo)
        b_last_col = _dot_tn(hi, ones) + _dot_tn(lo, ones)
        b_last_row = b_cum[GLA_CHUNK - 1:GLA_CHUNK, :]
        e_pos = jnp.exp(b_cum)
        e_neg = jnp.exp(-b_cum)
        e_end = jnp.exp(b_last_row - b_cum)
        for h in range(GLA_HEADS):
            ksl = slice(h * dk, (h + 1) * dk)
            vsl = slice(h * dv, (h + 1) * dv)
            q = qk_ref[rows, ksl]
            k = qk_ref[rows, kw + h * dk:kw + (h + 1) * dk]
            q_dec = (q * q_scale * e_pos[:, ksl]).astype(BF16)
            k_dec = (k * e_neg[:, ksl]).astype(BF16)
            k_end = (k * e_end[:, ksl]).astype(BF16)
            v = v_ref[rows, vsl]
            scores = jnp.where(causal, _dot_nt(q_dec, k_dec), 0.0).astype(BF16)
            state = state_ref[h]
            o = _dot(scores, v) + _dot(q_dec, state.astype(BF16))
            decay = jnp.exp(b_last_col[ksl, :])
            decay = jnp.concatenate([decay] * (dv // V7X_LANES), axis=1)
            state_ref[h] = state * decay + _dot_tn(k_end, v)
            ms = jnp.mean(o * o, axis=-1, keepdims=True)
            o = o * lax.rsqrt(ms + EPS) * ng_ref[...]
            o_ref[rows, vsl] = (_silu(r_ref[rows, vsl]) * o).astype(o_ref.dtype)
        return carry

    lax.fori_loop(0, chunks, chunk, 0)


def _gla(qk, v, r, g_low, w_gate_up, b_gate, norm_g):
    s = qk.shape[0]
    kw = w_gate_up.shape[1]
    vw = v.shape[1]
    dk, dv = kw // GLA_HEADS, vw // GLA_HEADS
    rows = _tile(s, 256)
    chunks = rows // GLA_CHUNK
    tril = jnp.tril(jnp.ones((GLA_CHUNK, GLA_CHUNK), BF16))
    return pl.pallas_call(
        functools.partial(_gla_kernel, chunks=chunks, dk=dk, dv=dv),
        grid=(s // rows,),
        in_specs=[pl.BlockSpec((rows, 2 * kw), lambda i: (i, 0)),
                  pl.BlockSpec((rows, vw), lambda i: (i, 0)),
                  pl.BlockSpec((rows, vw), lambda i: (i, 0)),
                  pl.BlockSpec((rows, V7X_LANES), lambda i: (i, 0)),
                  pl.BlockSpec((V7X_LANES, kw), lambda i: (0, 0)),
                  pl.BlockSpec((1, kw), lambda i: (0, 0)),
                  pl.BlockSpec((1, dv), lambda i: (0, 0)),
                  pl.BlockSpec((GLA_CHUNK, GLA_CHUNK), lambda i: (0, 0))],
        out_specs=pl.BlockSpec((rows, vw), lambda i: (i, 0)),
        out_shape=jax.ShapeDtypeStruct((s, vw), BF16),
        scratch_shapes=[pltpu.VMEM((GLA_HEADS, dk, dv), F32)],
        compiler_params=_params("arbitrary"),
        name="gla",
    )(qk, v, r, g_low, w_gate_up, b_gate.reshape(1, kw), norm_g.reshape(1, dv), tril)


def _sb_kernel(q_ref, k_ref, v_ref, t_ref, o_ref, acc_ref, run_ref, *, tq, tk):
    qi = pl.program_id(1)
    tmat = t_ref[...]
    chains = tq // tk
    base = qi * chains
    below_diag = (lax.broadcasted_iota(jnp.int32, (tk, tk), 1)
                  < lax.broadcasted_iota(jnp.int32, (tk, tk), 0))

    all_chains = range(chains)
    rows = [slice(c * tk, (c + 1) * tk) for c in all_chains]

    def sweep(blocks, diagonal, live):
        starts = [pl.multiple_of(j * tk, tk) for j in blocks]
        zs = [_dot_nt(q_ref[rows[c], :], k_ref[pl.ds(starts[c], tk), :])
              for c in all_chains]
        log_betas, halves = [], []
        for z in zs:
            softplus = jnp.maximum(z, 0.0) + jnp.log(1.0 + jnp.exp(-jnp.abs(z)))
            log_betas.append(z - softplus)
            neg_log_rest = softplus
            if diagonal:
                neg_log_rest = jnp.where(below_diag, neg_log_rest, 0.0)
            halves.append(jnp.concatenate(_split_bf16(neg_log_rest), axis=1))
        cms = [_dot(hl, tmat) for hl in halves]
        atts, runs = [], []
        for c in all_chains:
            log_att = log_betas[c] + cms[c][:, :tk]
            if diagonal:
                atts.append(jnp.where(below_diag, jnp.exp(log_att), 0.0))
                runs.append(cms[c][:, tk:])
            else:
                run = jnp.where(live[c], run_ref[rows[c], :], -1e30)
                atts.append(jnp.exp(log_att + run))
                runs.append(run + cms[c][:, tk:])
        pvs = [_dot(atts[c].astype(BF16), v_ref[pl.ds(starts[c], tk), :])
               for c in all_chains]
        if not diagonal:
            pvs = [acc_ref[rows[c], :] + pvs[c] for c in all_chains]
        for c in all_chains:
            acc_ref[rows[c], :] = pvs[c]
            run_ref[rows[c], :] = runs[c]
        return jnp.max(functools.reduce(jnp.maximum, runs))

    top = sweep([base + c for c in all_chains], True, None)

    def cond(st):
        n, top = st
        return jnp.logical_and(n <= base + chains - 1, top > F32_EXP_UNDERFLOW)

    def body(st):
        n, _ = st
        blocks = [base + c - n for c in all_chains]
        top = sweep([jnp.maximum(j, 0) for j in blocks], False, [j >= 0 for j in blocks])
        return n + 1, top

    lax.while_loop(cond, body, (jnp.int32(1), top))
    o_ref[...] = acc_ref[...].astype(o_ref.dtype)


def _sb_attention(q, k, v):
    s, d = q.shape
    dh = SB_HEAD_DIM
    tk = V7X_LANES
    tq = _tile(s, 1024)
    r = lax.broadcasted_iota(jnp.int32, (tk, 2 * tk), 0)
    c = lax.broadcasted_iota(jnp.int32, (tk, 2 * tk), 1)
    half = jnp.where(jnp.logical_or(c >= tk, r > c), -1.0, 0.0).astype(BF16)
    tmat = jnp.concatenate([half, half], axis=0)
    return pl.pallas_call(
        functools.partial(_sb_kernel, tq=tq, tk=tk),
        grid=(d // dh, s // tq),
        in_specs=[pl.BlockSpec((tq, dh), lambda h, i: (i, h)),
                  pl.BlockSpec((s, dh), lambda h, i: (0, h)),
                  pl.BlockSpec((s, dh), lambda h, i: (0, h)),
                  pl.BlockSpec((2 * tk, 2 * tk), lambda h, i: (0, 0))],
        out_specs=pl.BlockSpec((tq, dh), lambda h, i: (i, h)),
        out_shape=jax.ShapeDtypeStruct((s, d), BF16),
        scratch_shapes=[pltpu.VMEM((tq, dh), F32), pltpu.VMEM((tq, tk), F32)],
        compiler_params=_params("arbitrary", "arbitrary"),
        name="sb_attention",
    )(q, k, v, tmat)


def _ffn_block(x, mod, l, ffn_w_in, ffn_conv_w, ffn_conv_b, ffn_w_out_bf):
    d = x.shape[1]
    shift, scale, gate = (mod[i * d:(i + 1) * d] for i in (3, 4, 5))
    h = _norm_affine(x, scale, shift, True)
    g = _ffn_in(h, ffn_w_in, ffn_conv_w, ffn_conv_b, l)
    return _mm_residual(g, ffn_w_out_bf, x, gate, layer=l)


def kernel(x, c, w_ada, b_ada, ada_table, gla_w_in, gla_w_gate_up, gla_b_gate, gla_norm_g, gla_w_out, kv_norm_g, w_kv, k_norm_g, sb_w_q, sb_q_norm_g, sb_w_out, ffn_w_in, ffn_conv_w, ffn_conv_b, ffn_w_out):
    batch, s, d = x.shape
    assert batch == 1, "adaLN modulation rows are built for a single sequence"
    depth = ada_table.shape[0]
    n_a = gla_w_in.shape[0]
    kw = gla_w_gate_up.shape[2]
    rank = gla_w_gate_up.shape[1]
    x = x.reshape(s, d)

    gla_w_in_bf = gla_w_in.astype(BF16)
    gla_w_out_bf = gla_w_out.astype(BF16)
    w_kv_bf = w_kv.astype(BF16)
    sb_w_q_bf = sb_w_q.astype(BF16)
    sb_w_out_bf = sb_w_out.astype(BF16)
    ffn_w_out_bf = ffn_w_out.astype(BF16)
    w_up_bf = jnp.pad(gla_w_gate_up, ((0, 0), (0, V7X_LANES - rank), (0, 0))).astype(BF16)

    mods = _cond(c, w_ada, b_ada, ada_table)
    k_sh = v_sh = None
    for l in range(depth):
        mod = mods[l]
        shift, scale, gate = (mod[i * d:(i + 1) * d] for i in (0, 1, 2))
        h = _norm_affine(x, scale, shift, True)
        if l < n_a:
            qk = _mm_plain(h, gla_w_in_bf, F32, layer=l, col0=0, n=2 * kw)
            v = _mm_plain(h, gla_w_in_bf, BF16, layer=l, col0=2 * kw, n=d)
            r = _mm_plain(h, gla_w_in_bf, F32, layer=l, col0=2 * kw + d, n=d)
            g_low = _mm_plain(h, gla_w_in_bf, F32, layer=l, col0=2 * kw + 2 * d,
                              n=V7X_LANES, valid_cols=rank)
            o = _gla(qk, v, r, g_low, w_up_bf[l], gla_b_gate[l], gla_norm_g[l])
            x = _mm_residual(o, gla_w_out_bf, x, gate, layer=l)
        else:
            if l == n_a:
                src = _norm_affine(x, kv_norm_g, jnp.zeros_like(kv_norm_g), False)
                k_sh = _mm_headnorm(src, w_kv_bf, k_norm_g, 1.0, col0=0, n=d)
                v_sh = _mm_plain(src, w_kv_bf, BF16, col0=d, n=d)
            j = l - n_a
            q = _mm_headnorm(h, sb_w_q_bf, sb_q_norm_g[j], SB_HEAD_DIM ** -0.5, layer=j)
            o = _sb_attention(q, k_sh, v_sh)
            x = _mm_residual(o, sb_w_out_bf, x, gate, layer=j)
        x = _ffn_block(x, mod, l, ffn_w_in, ffn_conv_w, ffn_conv_b, ffn_w_out_bf)
    return x.reshape(batch, s, d)
```

```python
import functools

import jax
import jax.numpy as jnp
from jax import lax
from jax.experimental import pallas as pl
from jax.experimental.pallas import tpu as pltpu

F32 = jnp.float32
BF16 = jnp.bfloat16

EPS = 1e-6
N_MOD = 6
CONV_WIDTH = 3
GLA_CHUNK = 64
GLA_HEADS = 4
GLA_TAU = 16.0
SB_HEAD_DIM = 128
SB_HEADS_PER_STEP = 2

V7X_LANES = 128
V7X_SUBLANES = 8
V7X_VMEM_LIMIT_BYTES = 56 * 1024 * 1024
V7X_VMEM_LIMIT_LARGE_BYTES = 60 * 1024 * 1024

F32_EXP_UNDERFLOW = -104.0


def _params(*semantics, vmem_limit_bytes=V7X_VMEM_LIMIT_BYTES):
    return pltpu.CompilerParams(dimension_semantics=semantics,
                                vmem_limit_bytes=vmem_limit_bytes)


def _tile(n, pref):
    if n <= pref:
        return n
    t = (pref // V7X_LANES) * V7X_LANES
    while t >= V7X_LANES:
        if n % t == 0:
            return t
        t -= V7X_LANES
    return n


def _dot(a, b):
    return jnp.dot(a, b, preferred_element_type=F32)


def _dot_nt(a, b):
    return lax.dot_general(a, b, (((1,), (1,)), ((), ())), preferred_element_type=F32)


def _dot_tn(a, b):
    return lax.dot_general(a, b, (((0,), (0,)), ((), ())), preferred_element_type=F32)


def _silu(x):
    return x / (1.0 + jnp.exp(-x))


def _split_bf16(x):
    hi = x.astype(BF16)
    lo = (x - hi.astype(F32)).astype(BF16)
    return hi, lo


def _cond_kernel(c_ref, w_ref, b_ref, tab_ref, o_ref):
    s = _silu(c_ref[...]).astype(BF16)
    acc = _dot(s, w_ref[...].astype(BF16))
    o_ref[...] = acc[0:1, :] + b_ref[...] + tab_ref[...]


def _cond(c, w_ada, b_ada, ada_table):
    d = c.shape[1]
    depth = ada_table.shape[0]
    n = w_ada.shape[1]
    tn = _tile(n, 512)
    c8 = jnp.broadcast_to(c, (V7X_SUBLANES, d))
    return pl.pallas_call(
        _cond_kernel,
        grid=(n // tn,),
        in_specs=[pl.BlockSpec((V7X_SUBLANES, d), lambda j: (0, 0)),
                  pl.BlockSpec((d, tn), lambda j: (0, j)),
                  pl.BlockSpec((1, tn), lambda j: (0, j)),
                  pl.BlockSpec((depth, tn), lambda j: (0, j))],
        out_specs=pl.BlockSpec((depth, tn), lambda j: (0, j)),
        out_shape=jax.ShapeDtypeStruct((depth, n), F32),
        compiler_params=_params("arbitrary"),
        name="cond",
    )(c8, w_ada, b_ada.reshape(1, n), ada_table.reshape(depth, n))


def _norm_affine_kernel(x_ref, mul_ref, add_ref, o_ref, *, plus_one):
    x = x_ref[...]
    ms = jnp.mean(x * x, axis=-1, keepdims=True)
    xn = x * lax.rsqrt(ms + EPS)
    mul = mul_ref[...]
    if plus_one:
        mul = 1.0 + mul
    o_ref[...] = (xn * mul + add_ref[...]).astype(o_ref.dtype)


def _norm_affine(x, mul, add, plus_one):
    s, d = x.shape
    tm = _tile(s, 256)
    return pl.pallas_call(
        functools.partial(_norm_affine_kernel, plus_one=plus_one),
        grid=(s // tm,),
        in_specs=[pl.BlockSpec((tm, d), lambda i: (i, 0)),
                  pl.BlockSpec((1, d), lambda i: (0, 0)),
                  pl.BlockSpec((1, d), lambda i: (0, 0))],
        out_specs=pl.BlockSpec((tm, d), lambda i: (i, 0)),
        out_shape=jax.ShapeDtypeStruct((s, d), BF16),
        compiler_params=_params("arbitrary"),
        name="norm_affine",
    )(x, mul.reshape(1, d), add.reshape(1, d))


def _mm_plain_kernel(a_ref, w_ref, o_ref, *, valid_cols=None):
    w = w_ref[...]
    if valid_cols is not None:
        col = lax.broadcasted_iota(jnp.int32, w.shape, 1)
        w = jnp.where(col < valid_cols, w, jnp.zeros_like(w))
    o_ref[...] = _dot(a_ref[...], w).astype(o_ref.dtype)


def _mm_headnorm_kernel(a_ref, w_ref, g_ref, o_ref, *, post_scale):
    acc = _dot(a_ref[...], w_ref[...])
    g = g_ref[...] * post_scale
    for c in range(acc.shape[1] // SB_HEAD_DIM):
        sl = slice(c * SB_HEAD_DIM, (c + 1) * SB_HEAD_DIM)
        blk = acc[:, sl]
        ms = jnp.mean(blk * blk, axis=-1, keepdims=True)
        o_ref[:, sl] = (blk * lax.rsqrt(ms + EPS) * g).astype(o_ref.dtype)


def _mm_residual_kernel(a_ref, w_ref, x_ref, gate_ref, o_ref):
    o_ref[...] = x_ref[...] + gate_ref[...] * _dot(a_ref[...], w_ref[...])


def _mm_tiles(m, n, k):
    tm = _tile(m, 1024)
    tn = _tile(n, 1024 if k <= 4096 else 512)
    if k > 4096:
        tm = _tile(m, 512)
    return tm, tn


def _weight_spec(w, layer, col0, tn):
    k = w.shape[-2]
    assert col0 % tn == 0
    c0 = col0 // tn
    if w.ndim == 2:
        return pl.BlockSpec((k, tn), lambda i, j: (0, c0 + j))
    return pl.BlockSpec((None, k, tn), lambda i, j: (layer, 0, c0 + j))


def _mm_plain(a, w, out_dtype, *, layer=0, col0=0, n=None, valid_cols=None):
    m, k = a.shape
    n = w.shape[-1] if n is None else n
    tm, tn = _mm_tiles(m, n, k)
    return pl.pallas_call(
        functools.partial(_mm_plain_kernel, valid_cols=valid_cols),
        grid=(m // tm, n // tn),
        in_specs=[pl.BlockSpec((tm, k), lambda i, j: (i, 0)),
                  _weight_spec(w, layer, col0, tn)],
        out_specs=pl.BlockSpec((tm, tn), lambda i, j: (i, j)),
        out_shape=jax.ShapeDtypeStruct((m, n), out_dtype),
        compiler_params=_params("arbitrary", "arbitrary"),
        name="mm_plain",
    )(a, w)


def _mm_headnorm(a, w, g, post_scale, *, layer=0, col0=0, n=None):
    m, k = a.shape
    n = w.shape[-1] if n is None else n
    tm, tn = _mm_tiles(m, n, k)
    return pl.pallas_call(
        functools.partial(_mm_headnorm_kernel, post_scale=post_scale),
        grid=(m // tm, n // tn),
        in_specs=[pl.BlockSpec((tm, k), lambda i, j: (i, 0)),
                  _weight_spec(w, layer, col0, tn),
                  pl.BlockSpec((1, SB_HEAD_DIM), lambda i, j: (0, 0))],
        out_specs=pl.BlockSpec((tm, tn), lambda i, j: (i, j)),
        out_shape=jax.ShapeDtypeStruct((m, n), BF16),
        compiler_params=_params("arbitrary", "arbitrary"),
        name="mm_headnorm",
    )(a, w, g.reshape(1, SB_HEAD_DIM))


def _mm_residual(a, w, x, gate, *, layer=0):
    m, k = a.shape
    n = w.shape[-1]
    tm, tn = _mm_tiles(m, n, k)
    return pl.pallas_call(
        _mm_residual_kernel,
        grid=(m // tm, n // tn),
        in_specs=[pl.BlockSpec((tm, k), lambda i, j: (i, 0)),
                  _weight_spec(w, layer, 0, tn),
                  pl.BlockSpec((tm, tn), lambda i, j: (i, j)),
                  pl.BlockSpec((1, tn), lambda i, j: (0, j))],
        out_specs=pl.BlockSpec((tm, tn), lambda i, j: (i, j)),
        out_shape=jax.ShapeDtypeStruct((m, n), F32),
        compiler_params=_params("arbitrary", "arbitrary"),
        name="mm_residual",
    )(a, w, x, gate.reshape(1, n))


FFN_CHUNK_ROWS = 128


def _ffn_in_kernel(h_ref, wa_ref, wb_ref, cwa_ref, cwb_ref, cba_ref, cbb_ref,
                   o_ref, wa_bf_ref, wb_bf_ref, carry_a_ref, carry_b_ref, ua_ref, ub_ref):
    tm = h_ref.shape[0]
    cm = min(FFN_CHUNK_ROWS, tm)

    @pl.when(pl.program_id(1) == 0)
    def _():
        wa_bf_ref[...] = wa_ref[...].astype(BF16)
        wb_bf_ref[...] = wb_ref[...].astype(BF16)
        carry_a_ref[...] = jnp.zeros_like(carry_a_ref)
        carry_b_ref[...] = jnp.zeros_like(carry_b_ref)

    row = lax.broadcasted_iota(jnp.int32, carry_a_ref.shape, 0)

    def conv(u, prev, cw_ref, cb_ref):
        w0, w1, w2 = cw_ref[0:1, :], cw_ref[1:2, :], cw_ref[2:3, :]
        bias = cb_ref[...]
        u1 = pltpu.roll(u, 1, axis=0)
        u2 = pltpu.roll(u, 2, axis=0)
        head1 = jnp.where(row < 1, pltpu.roll(prev, 1, axis=0), u1[0:V7X_SUBLANES])
        head2 = jnp.where(row < 2, pltpu.roll(prev, 2, axis=0), u2[0:V7X_SUBLANES])
        head = bias + head2 * w0 + head1 * w1 + u[0:V7X_SUBLANES] * w2
        tail = (bias + u2[V7X_SUBLANES:] * w0 + u1[V7X_SUBLANES:] * w1
                + u[V7X_SUBLANES:] * w2)
        return head, tail

    prev_a = carry_a_ref[...]
    prev_b = carry_b_ref[...]
    for c in range(tm // cm):
        r0 = c * cm
        h = h_ref[r0:r0 + cm, :]
        slot = lax.rem(pl.program_id(1) + c, 2)
        ua_ref[slot] = _dot(h, wa_bf_ref[...])
        ub_ref[slot] = _dot(h, wb_bf_ref[...])
        ua = ua_ref[slot]
        ub = ub_ref[slot]
        a_head, a_tail = conv(ua, prev_a, cwa_ref, cba_ref)
        b_head, b_tail = conv(ub, prev_b, cwb_ref, cbb_ref)
        o_ref[r0:r0 + V7X_SUBLANES, :] = (_silu(a_head) * b_head).astype(o_ref.dtype)
        o_ref[r0 + V7X_SUBLANES:r0 + cm, :] = (_silu(a_tail) * b_tail).astype(o_ref.dtype)
        prev_a = ua[cm - V7X_SUBLANES:cm]
        prev_b = ub[cm - V7X_SUBLANES:cm]
    carry_a_ref[...] = prev_a
    carry_b_ref[...] = prev_b


def _ffn_in(h, w_in, conv_w, conv_b, layer):
    s, d = h.shape
    f = w_in.shape[-1] // 2
    tm = _tile(s, 2048)
    tn = _tile(f, 256)
    nj = f // tn
    cm = min(FFN_CHUNK_ROWS, tm)
    conv_b = conv_b.reshape(conv_b.shape[0], 1, 2 * f)
    return pl.pallas_call(
        _ffn_in_kernel,
        grid=(nj, s // tm),
        in_specs=[pl.BlockSpec((tm, d), lambda j, i: (i, 0)),
                  pl.BlockSpec((None, d, tn), lambda j, i: (layer, 0, j)),
                  pl.BlockSpec((None, d, tn), lambda j, i: (layer, 0, j + nj)),
                  pl.BlockSpec((None, CONV_WIDTH, tn), lambda j, i: (layer, 0, j)),
                  pl.BlockSpec((None, CONV_WIDTH, tn), lambda j, i: (layer, 0, j + nj)),
                  pl.BlockSpec((None, 1, tn), lambda j, i: (layer, 0, j)),
                  pl.BlockSpec((None, 1, tn), lambda j, i: (layer, 0, j + nj))],
        out_specs=pl.BlockSpec((tm, tn), lambda j, i: (i, j)),
        out_shape=jax.ShapeDtypeStruct((s, f), BF16),
        scratch_shapes=[pltpu.VMEM((d, tn), BF16),
                        pltpu.VMEM((d, tn), BF16),
                        pltpu.VMEM((V7X_SUBLANES, tn), F32),
                        pltpu.VMEM((V7X_SUBLANES, tn), F32),
                        pltpu.VMEM((2, cm, tn), F32),
                        pltpu.VMEM((2, cm, tn), F32)],
        compiler_params=_params("arbitrary", "arbitrary", vmem_limit_bytes=V7X_VMEM_LIMIT_LARGE_BYTES),
        name="ffn_in",
    )(h, w_in, w_in, conv_w, conv_w, conv_b, conv_b)


def _gla_kernel(qk_ref, v_ref, r_ref, gl_ref, wgu_ref, bg_ref, ng_ref, tril_ref,
                o_ref, state_ref, *, chunks, dk, dv):
    kw = GLA_HEADS * dk

    @pl.when(pl.program_id(0) == 0)
    def _():
        state_ref[...] = jnp.zeros_like(state_ref)

    tril = tril_ref[...]
    causal = tril > 0
    q_scale = dk ** -0.5

    def chunk(c, carry):
        rows = pl.ds(pl.multiple_of(c * GLA_CHUNK, GLA_CHUNK), GLA_CHUNK)
        y = _dot(gl_ref[rows, :].astype(BF16), wgu_ref[...]) + bg_ref[...]
        log_alpha = (jnp.minimum(y, 0.0) - jnp.log(1.0 + jnp.exp(-jnp.abs(y)))) * (1.0 / GLA_TAU)
        hi, lo = _split_bf16(log_alpha)
        b_cum = _dot(tril, hi) + _dot(tril, lo)
        b_last_row = b_cum[GLA_CHUNK - 1:GLA_CHUNK, :]
        e_pos = jnp.exp(b_cum)
        e_neg = jnp.exp(-b_cum)
        e_end = jnp.exp(b_last_row - b_cum)
        decay_rows = jnp.broadcast_to(jnp.exp(b_last_row), (V7X_LANES, kw))
        for h in range(GLA_HEADS):
            ksl = slice(h * dk, (h + 1) * dk)
            vsl = slice(h * dv, (h + 1) * dv)
            q = qk_ref[rows, ksl]
            k = qk_ref[rows, kw + h * dk:kw + (h + 1) * dk]
            q_dec = (q * q_scale * e_pos[:, ksl]).astype(BF16)
            k_dec = (k * e_neg[:, ksl]).astype(BF16)
            k_end = (k * e_end[:, ksl]).astype(BF16)
            v = v_ref[rows, vsl]
            scores = jnp.where(causal, _dot_nt(q_dec, k_dec), 0.0).astype(BF16)
            state = state_ref[h]
            o = _dot(scores, v) + _dot(q_dec, state.astype(BF16))
            decay = jnp.concatenate(
                [decay_rows[:, h * dk + b * V7X_LANES:h * dk + (b + 1) * V7X_LANES].T
                 for b in range(dk // V7X_LANES)], axis=0)
            decay = jnp.concatenate([decay] * (dv // V7X_LANES), axis=1)
            state_ref[h] = state * decay + _dot_tn(k_end, v)
            ms = jnp.mean(o * o, axis=-1, keepdims=True)
            o = o * lax.rsqrt(ms + EPS) * ng_ref[...]
            o_ref[rows, vsl] = (_silu(r_ref[rows, vsl]) * o).astype(o_ref.dtype)
        return carry

    lax.fori_loop(0, chunks, chunk, 0)


def _gla(qk, v, r, g_low, w_gate_up, b_gate, norm_g):
    s = qk.shape[0]
    kw = w_gate_up.shape[1]
    vw = v.shape[1]
    dk, dv = kw // GLA_HEADS, vw // GLA_HEADS
    rows = _tile(s, 256)
    chunks = rows // GLA_CHUNK
    tril = jnp.tril(jnp.ones((GLA_CHUNK, GLA_CHUNK), BF16))
    return pl.pallas_call(
        functools.partial(_gla_kernel, chunks=chunks, dk=dk, dv=dv),
        grid=(s // rows,),
        in_specs=[pl.BlockSpec((rows, 2 * kw), lambda i: (i, 0)),
                  pl.BlockSpec((rows, vw), lambda i: (i, 0)),
                  pl.BlockSpec((rows, vw), lambda i: (i, 0)),
                  pl.BlockSpec((rows, V7X_LANES), lambda i: (i, 0)),
                  pl.BlockSpec((V7X_LANES, kw), lambda i: (0, 0)),
                  pl.BlockSpec((1, kw), lambda i: (0, 0)),
                  pl.BlockSpec((1, dv), lambda i: (0, 0)),
                  pl.BlockSpec((GLA_CHUNK, GLA_CHUNK), lambda i: (0, 0))],
        out_specs=pl.BlockSpec((rows, vw), lambda i: (i, 0)),
        out_shape=jax.ShapeDtypeStruct((s, vw), BF16),
        scratch_shapes=[pltpu.VMEM((GLA_HEADS, dk, dv), F32)],
        compiler_params=_params("arbitrary"),
        name="gla",
    )(qk, v, r, g_low, w_gate_up, b_gate.reshape(1, kw), norm_g.reshape(1, dv), tril)


def _sb_kernel(q_ref, k_ref, v_ref, t_ref, o_ref, acc_ref, run_ref, *, tq, tk, heads):
    qi = pl.program_id(1)
    tmat = t_ref[...]
    row_groups = tq // tk
    base = qi * row_groups
    below_diag = (lax.broadcasted_iota(jnp.int32, (tk, tk), 1)
                  < lax.broadcasted_iota(jnp.int32, (tk, tk), 0))

    chains = heads * row_groups
    all_chains = range(chains)
    group = [c % row_groups for c in all_chains]
    rows = [slice(g * tk, (g + 1) * tk) for g in group]
    lanes = [slice((c // row_groups) * SB_HEAD_DIM, (c // row_groups + 1) * SB_HEAD_DIM)
             for c in all_chains]

    def sweep(blocks, diagonal, live):
        starts = [pl.multiple_of(blocks[g] * tk, tk) for g in group]
        if live is not None:
            live = [live[g] for g in group]
        zs = [_dot_nt(q_ref[rows[c], lanes[c]], k_ref[pl.ds(starts[c], tk), lanes[c]])
              for c in all_chains]
        log_betas, halves = [], []
        for z in zs:
            softplus = jnp.maximum(z, 0.0) + jnp.log(1.0 + jnp.exp(-jnp.abs(z)))
            log_betas.append(z - softplus)
            neg_log_rest = softplus
            if diagonal:
                neg_log_rest = jnp.where(below_diag, neg_log_rest, 0.0)
            halves.append(jnp.concatenate(_split_bf16(neg_log_rest), axis=1))
        cms = [_dot(hl, tmat) for hl in halves]
        atts, runs = [], []
        for c in all_chains:
            log_att = log_betas[c] + cms[c][:, :tk]
            if diagonal:
                atts.append(jnp.where(below_diag, jnp.exp(log_att), 0.0))
                runs.append(cms[c][:, tk:])
            else:
                run = jnp.where(live[c], run_ref[rows[c], lanes[c]], -1e30)
                atts.append(jnp.exp(log_att + run))
                runs.append(run + cms[c][:, tk:])
        pvs = [_dot(atts[c].astype(BF16), v_ref[pl.ds(starts[c], tk), lanes[c]])
               for c in all_chains]
        if not diagonal:
            pvs = [acc_ref[rows[c], lanes[c]] + pvs[c] for c in all_chains]
        for c in all_chains:
            acc_ref[rows[c], lanes[c]] = pvs[c]
            run_ref[rows[c], lanes[c]] = runs[c]
        return jnp.max(functools.reduce(jnp.maximum, runs))

    top = sweep([base + g for g in range(row_groups)], True, None)

    def cond(st):
        n, top = st
        return jnp.logical_and(n <= base + row_groups - 1, top > F32_EXP_UNDERFLOW)

    def body(st):
        n, _ = st
        blocks = [base + g - n for g in range(row_groups)]
        top = sweep([jnp.maximum(j, 0) for j in blocks], False, [j >= 0 for j in blocks])
        return n + 1, top

    lax.while_loop(cond, body, (jnp.int32(1), top))
    o_ref[...] = acc_ref[...].astype(o_ref.dtype)


def _sb_attention(q, k, v):
    s, d = q.shape
    dh = SB_HEAD_DIM
    tk = V7X_LANES
    tq = _tile(s, 1024)
    r = lax.broadcasted_iota(jnp.int32, (tk, 2 * tk), 0)
    c = lax.broadcasted_iota(jnp.int32, (tk, 2 * tk), 1)
    half = jnp.where(jnp.logical_or(c >= tk, r > c), -1.0, 0.0).astype(BF16)
    tmat = jnp.concatenate([half, half], axis=0)
    heads = SB_HEADS_PER_STEP
    wh = heads * dh
    return pl.pallas_call(
        functools.partial(_sb_kernel, tq=tq, tk=tk, heads=heads),
        grid=(d // wh, s // tq),
        in_specs=[pl.BlockSpec((tq, wh), lambda h, i: (i, h)),
                  pl.BlockSpec((s, wh), lambda h, i: (0, h)),
                  pl.BlockSpec((s, wh), lambda h, i: (0, h)),
                  pl.BlockSpec((2 * tk, 2 * tk), lambda h, i: (0, 0))],
        out_specs=pl.BlockSpec((tq, wh), lambda h, i: (i, h)),
        out_shape=jax.ShapeDtypeStruct((s, d), BF16),
        scratch_shapes=[pltpu.VMEM((tq, wh), F32), pltpu.VMEM((tq, wh), F32)],
        compiler_params=_params("arbitrary", "arbitrary"),
        name="sb_attention",
    )(q, k, v, tmat)


def _ffn_block(x, mod, l, ffn_w_in, ffn_conv_w, ffn_conv_b, ffn_w_out_bf):
    d = x.shape[1]
    shift, scale, gate = (mod[i * d:(i + 1) * d] for i in (3, 4, 5))
    h = _norm_affine(x, scale, shift, True)
    g = _ffn_in(h, ffn_w_in, ffn_conv_w, ffn_conv_b, l)
    return _mm_residual(g, ffn_w_out_bf, x, gate, layer=l)


def kernel(x, c, w_ada, b_ada, ada_table, gla_w_in, gla_w_gate_up, gla_b_gate, gla_norm_g, gla_w_out, kv_norm_g, w_kv, k_norm_g, sb_w_q, sb_q_norm_g, sb_w_out, ffn_w_in, ffn_conv_w, ffn_conv_b, ffn_w_out):
    batch, s, d = x.shape
    assert batch == 1, "adaLN modulation rows are built for a single sequence"
    depth = ada_table.shape[0]
    n_a = gla_w_in.shape[0]
    kw = gla_w_gate_up.shape[2]
    rank = gla_w_gate_up.shape[1]
    x = x.reshape(s, d)

    gla_w_in_bf = gla_w_in.astype(BF16)
    gla_w_out_bf = gla_w_out.astype(BF16)
    w_kv_bf = w_kv.astype(BF16)
    sb_w_q_bf = sb_w_q.astype(BF16)
    sb_w_out_bf = sb_w_out.astype(BF16)
    ffn_w_out_bf = ffn_w_out.astype(BF16)
    w_up_bf = jnp.pad(gla_w_gate_up, ((0, 0), (0, V7X_LANES - rank), (0, 0))).astype(BF16)

    mods = _cond(c, w_ada, b_ada, ada_table)
    k_sh = v_sh = None
    for l in range(depth):
        mod = mods[l]
        shift, scale, gate = (mod[i * d:(i + 1) * d] for i in (0, 1, 2))
        h = _norm_affine(x, scale, shift, True)
        if l < n_a:
            qk = _mm_plain(h, gla_w_in_bf, F32, layer=l, col0=0, n=2 * kw)
            v = _mm_plain(h, gla_w_in_bf, BF16, layer=l, col0=2 * kw, n=d)
            r = _mm_plain(h, gla_w_in_bf, F32, layer=l, col0=2 * kw + d, n=d)
            g_low = _mm_plain(h, gla_w_in_bf, F32, layer=l, col0=2 * kw + 2 * d,
                              n=V7X_LANES, valid_cols=rank)
            o = _gla(qk, v, r, g_low, w_up_bf[l], gla_b_gate[l], gla_norm_g[l])
            x = _mm_residual(o, gla_w_out_bf, x, gate, layer=l)
        else:
            if l == n_a:
                src = _norm_affine(x, kv_norm_g, jnp.zeros_like(kv_norm_g), False)
                k_sh = _mm_headnorm(src, w_kv_bf, k_norm_g, 1.0, col0=0, n=d)
                v_sh = _mm_plain(src, w_kv_bf, BF16, col0=d, n=d)
            j = l - n_a
            q = _mm_headnorm(h, sb_w_q_bf, sb_q_norm_g[j], SB_HEAD_DIM ** -0.5, layer=j)
            o = _sb_attention(q, k_sh, v_sh)
            x = _mm_residual(o, sb_w_out_bf, x, gate, layer=j)
        x = _ffn_block(x, mod, l, ffn_w_in, ffn_conv_w, ffn_conv_b, ffn_w_out_bf)
    return x.reshape(batch, s, d)
```

```python
import functools

import jax
import jax.numpy as jnp
from jax import lax
from jax.experimental import pallas as pl
from jax.experimental.pallas import tpu as pltpu

F32 = jnp.float32
BF16 = jnp.bfloat16

EPS = 1e-6
N_MOD = 6
CONV_WIDTH = 3
GLA_CHUNK = 64
GLA_HEADS = 4
GLA_TAU = 16.0
SB_HEAD_DIM = 128
SB_HEADS_PER_STEP = 2

V7X_LANES = 128
V7X_SUBLANES = 8
V7X_VMEM_LIMIT_BYTES = 56 * 1024 * 1024

F32_EXP_UNDERFLOW = -88.0


def _params(*semantics):
    return pltpu.CompilerParams(dimension_semantics=semantics,
                                vmem_limit_bytes=V7X_VMEM_LIMIT_BYTES)


def _tile(n, pref):
    if n <= pref:
        return n
    t = (pref // V7X_LANES) * V7X_LANES
    while t >= V7X_LANES:
        if n % t == 0:
            return t
        t -= V7X_LANES
    return n


def _dot(a, b):
    return jnp.dot(a, b, preferred_element_type=F32)


def _dot_nt(a, b):
    return lax.dot_general(a, b, (((1,), (1,)), ((), ())), preferred_element_type=F32)


def _dot_tn(a, b):
    return lax.dot_general(a, b, (((0,), (0,)), ((), ())), preferred_element_type=F32)


def _silu(x):
    return x / (1.0 + jnp.exp(-x))


def _split_bf16(x):
    hi = x.astype(BF16)
    lo = (x - hi.astype(F32)).astype(BF16)
    return hi, lo


def _cond_kernel(c_ref, w_ref, b_ref, tab_ref, o_ref):
    s = _silu(c_ref[...]).astype(BF16)
    acc = _dot(s, w_ref[...].astype(BF16))
    o_ref[...] = acc[0:1, :] + b_ref[...] + tab_ref[...]


def _cond(c, w_ada, b_ada, ada_table):
    d = c.shape[1]
    depth = ada_table.shape[0]
    n = w_ada.shape[1]
    tn = _tile(n, 512)
    c8 = jnp.broadcast_to(c, (V7X_SUBLANES, d))
    return pl.pallas_call(
        _cond_kernel,
        grid=(n // tn,),
        in_specs=[pl.BlockSpec((V7X_SUBLANES, d), lambda j: (0, 0)),
                  pl.BlockSpec((d, tn), lambda j: (0, j)),
                  pl.BlockSpec((1, tn), lambda j: (0, j)),
                  pl.BlockSpec((depth, tn), lambda j: (0, j))],
        out_specs=pl.BlockSpec((depth, tn), lambda j: (0, j)),
        out_shape=jax.ShapeDtypeStruct((depth, n), F32),
        compiler_params=_params("arbitrary"),
        name="cond",
    )(c8, w_ada, b_ada.reshape(1, n), ada_table.reshape(depth, n))


def _norm_affine_kernel(x_ref, mul_ref, add_ref, o_ref, *, plus_one):
    x = x_ref[...]
    ms = jnp.mean(x * x, axis=-1, keepdims=True)
    xn = x * lax.rsqrt(ms + EPS)
    mul = mul_ref[...]
    if plus_one:
        mul = 1.0 + mul
    o_ref[...] = (xn * mul + add_ref[...]).astype(o_ref.dtype)


def _norm_affine(x, mul, add, plus_one):
    s, d = x.shape
    tm = _tile(s, 256)
    return pl.pallas_call(
        functools.partial(_norm_affine_kernel, plus_one=plus_one),
        grid=(s // tm,),
        in_specs=[pl.BlockSpec((tm, d), lambda i: (i, 0)),
                  pl.BlockSpec((1, d), lambda i: (0, 0)),
                  pl.BlockSpec((1, d), lambda i: (0, 0))],
        out_specs=pl.BlockSpec((tm, d), lambda i: (i, 0)),
        out_shape=jax.ShapeDtypeStruct((s, d), BF16),
        compiler_params=_params("arbitrary"),
        name="norm_affine",
    )(x, mul.reshape(1, d), add.reshape(1, d))


def _mm_plain_kernel(a_ref, w_ref, o_ref, *, valid_cols=None):
    w = w_ref[...]
    if valid_cols is not None:
        col = lax.broadcasted_iota(jnp.int32, w.shape, 1)
        w = jnp.where(col < valid_cols, w, jnp.zeros_like(w))
    o_ref[...] = _dot(a_ref[...], w).astype(o_ref.dtype)


def _mm_headnorm_kernel(a_ref, w_ref, g_ref, o_ref, *, post_scale):
    acc = _dot(a_ref[...], w_ref[...])
    g = g_ref[...] * post_scale
    for c in range(acc.shape[1] // SB_HEAD_DIM):
        sl = slice(c * SB_HEAD_DIM, (c + 1) * SB_HEAD_DIM)
        blk = acc[:, sl]
        ms = jnp.mean(blk * blk, axis=-1, keepdims=True)
        o_ref[:, sl] = (blk * lax.rsqrt(ms + EPS) * g).astype(o_ref.dtype)


def _mm_residual_kernel(a_ref, w_ref, x_ref, gate_ref, o_ref):
    o_ref[...] = x_ref[...] + gate_ref[...] * _dot(a_ref[...], w_ref[...])


def _mm_tiles(m, n, k):
    tm = _tile(m, 1024)
    tn = _tile(n, 1024 if k <= 4096 else 512)
    if k > 4096:
        tm = _tile(m, 512)
    return tm, tn


def _weight_spec(w, layer, col0, tn):
    k = w.shape[-2]
    assert col0 % tn == 0
    c0 = col0 // tn
    if w.ndim == 2:
        return pl.BlockSpec((k, tn), lambda i, j: (0, c0 + j))
    return pl.BlockSpec((None, k, tn), lambda i, j: (layer, 0, c0 + j))


def _mm_plain(a, w, out_dtype, *, layer=0, col0=0, n=None, valid_cols=None):
    m, k = a.shape
    n = w.shape[-1] if n is None else n
    tm, tn = _mm_tiles(m, n, k)
    return pl.pallas_call(
        functools.partial(_mm_plain_kernel, valid_cols=valid_cols),
        grid=(m // tm, n // tn),
        in_specs=[pl.BlockSpec((tm, k), lambda i, j: (i, 0)),
                  _weight_spec(w, layer, col0, tn)],
        out_specs=pl.BlockSpec((tm, tn), lambda i, j: (i, j)),
        out_shape=jax.ShapeDtypeStruct((m, n), out_dtype),
        compiler_params=_params("arbitrary", "arbitrary"),
        name="mm_plain",
    )(a, w)


def _mm_headnorm(a, w, g, post_scale, *, layer=0, col0=0, n=None):
    m, k = a.shape
    n = w.shape[-1] if n is None else n
    tm, tn = _mm_tiles(m, n, k)
    return pl.pallas_call(
        functools.partial(_mm_headnorm_kernel, post_scale=post_scale),
        grid=(m // tm, n // tn),
        in_specs=[pl.BlockSpec((tm, k), lambda i, j: (i, 0)),
                  _weight_spec(w, layer, col0, tn),
                  pl.BlockSpec((1, SB_HEAD_DIM), lambda i, j: (0, 0))],
        out_specs=pl.BlockSpec((tm, tn), lambda i, j: (i, j)),
        out_shape=jax.ShapeDtypeStruct((m, n), BF16),
        compiler_params=_params("arbitrary", "arbitrary"),
        name="mm_headnorm",
    )(a, w, g.reshape(1, SB_HEAD_DIM))


def _mm_residual(a, w, x, gate, *, layer=0):
    m, k = a.shape
    n = w.shape[-1]
    tm, tn = _mm_tiles(m, n, k)
    return pl.pallas_call(
        _mm_residual_kernel,
        grid=(m // tm, n // tn),
        in_specs=[pl.BlockSpec((tm, k), lambda i, j: (i, 0)),
                  _weight_spec(w, layer, 0, tn),
                  pl.BlockSpec((tm, tn), lambda i, j: (i, j)),
                  pl.BlockSpec((1, tn), lambda i, j: (0, j))],
        out_specs=pl.BlockSpec((tm, tn), lambda i, j: (i, j)),
        out_shape=jax.ShapeDtypeStruct((m, n), F32),
        compiler_params=_params("arbitrary", "arbitrary"),
        name="mm_residual",
    )(a, w, x, gate.reshape(1, n))


FFN_CHUNK_ROWS = 128


def _ffn_in_kernel(h_ref, wa_ref, wb_ref, cwa_ref, cwb_ref, cba_ref, cbb_ref,
                   o_ref, wa_bf_ref, wb_bf_ref, carry_a_ref, carry_b_ref, ua_ref, ub_ref):
    tm = h_ref.shape[0]
    cm = min(FFN_CHUNK_ROWS, tm)

    @pl.when(pl.program_id(1) == 0)
    def _():
        wa_bf_ref[...] = wa_ref[...].astype(BF16)
        wb_bf_ref[...] = wb_ref[...].astype(BF16)
        carry_a_ref[...] = jnp.zeros_like(carry_a_ref)
        carry_b_ref[...] = jnp.zeros_like(carry_b_ref)

    row = lax.broadcasted_iota(jnp.int32, carry_a_ref.shape, 0)

    def conv(u, prev, cw_ref, cb_ref):
        w0, w1, w2 = cw_ref[0:1, :], cw_ref[1:2, :], cw_ref[2:3, :]
        bias = cb_ref[...]
        u1 = pltpu.roll(u, 1, axis=0)
        u2 = pltpu.roll(u, 2, axis=0)
        head1 = jnp.where(row < 1, pltpu.roll(prev, 1, axis=0), u1[0:V7X_SUBLANES])
        head2 = jnp.where(row < 2, pltpu.roll(prev, 2, axis=0), u2[0:V7X_SUBLANES])
        head = bias + head2 * w0 + head1 * w1 + u[0:V7X_SUBLANES] * w2
        tail = (bias + u2[V7X_SUBLANES:] * w0 + u1[V7X_SUBLANES:] * w1
                + u[V7X_SUBLANES:] * w2)
        return head, tail

    prev_a = carry_a_ref[...]
    prev_b = carry_b_ref[...]
    for c in range(tm // cm):
        r0 = c * cm
        h = h_ref[r0:r0 + cm, :]
        slot = lax.rem(pl.program_id(1) + c, 2)
        ua_ref[slot] = _dot(h, wa_bf_ref[...])
        ub_ref[slot] = _dot(h, wb_bf_ref[...])
        ua = ua_ref[slot]
        ub = ub_ref[slot]
        a_head, a_tail = conv(ua, prev_a, cwa_ref, cba_ref)
        b_head, b_tail = conv(ub, prev_b, cwb_ref, cbb_ref)
        o_ref[r0:r0 + V7X_SUBLANES, :] = (_silu(a_head) * b_head).astype(o_ref.dtype)
        o_ref[r0 + V7X_SUBLANES:r0 + cm, :] = (_silu(a_tail) * b_tail).astype(o_ref.dtype)
        prev_a = ua[cm - V7X_SUBLANES:cm]
        prev_b = ub[cm - V7X_SUBLANES:cm]
    carry_a_ref[...] = prev_a
    carry_b_ref[...] = prev_b


def _ffn_in(h, w_in, conv_w, conv_b, layer):
    s, d = h.shape
    f = w_in.shape[-1] // 2
    tm = _tile(s, 1024)
    tn = _tile(f, 256)
    nj = f // tn
    cm = min(FFN_CHUNK_ROWS, tm)
    conv_b = conv_b.reshape(conv_b.shape[0], 1, 2 * f)
    return pl.pallas_call(
        _ffn_in_kernel,
        grid=(nj, s // tm),
        in_specs=[pl.BlockSpec((tm, d), lambda j, i: (i, 0)),
                  pl.BlockSpec((None, d, tn), lambda j, i: (layer, 0, j)),
                  pl.BlockSpec((None, d, tn), lambda j, i: (layer, 0, j + nj)),
                  pl.BlockSpec((None, CONV_WIDTH, tn), lambda j, i: (layer, 0, j)),
                  pl.BlockSpec((None, CONV_WIDTH, tn), lambda j, i: (layer, 0, j + nj)),
                  pl.BlockSpec((None, 1, tn), lambda j, i: (layer, 0, j)),
                  pl.BlockSpec((None, 1, tn), lambda j, i: (layer, 0, j + nj))],
        out_specs=pl.BlockSpec((tm, tn), lambda j, i: (i, j)),
        out_shape=jax.ShapeDtypeStruct((s, f), BF16),
        scratch_shapes=[pltpu.VMEM((d, tn), BF16),
                        pltpu.VMEM((d, tn), BF16),
                        pltpu.VMEM((V7X_SUBLANES, tn), F32),
                        pltpu.VMEM((V7X_SUBLANES, tn), F32),
                        pltpu.VMEM((2, cm, tn), F32),
                        pltpu.VMEM((2, cm, tn), F32)],
        compiler_params=_params("arbitrary", "arbitrary"),
        name="ffn_in",
    )(h, w_in, w_in, conv_w, conv_w, conv_b, conv_b)


def _gla_kernel(qk_ref, v_ref, r_ref, gl_ref, wgu_ref, bg_ref, ng_ref, tril_ref,
                o_ref, state_ref, *, pairs, dk, dv):
    kw = GLA_HEADS * dk
    pair_rows = 2 * GLA_CHUNK

    @pl.when(pl.program_id(0) == 0)
    def _():
        state_ref[...] = jnp.zeros_like(state_ref)

    tril = tril_ref[...]
    own_causal = tril > 0
    r_id = lax.broadcasted_iota(jnp.int32, (pair_rows, pair_rows), 0)
    c_id = lax.broadcasted_iota(jnp.int32, (pair_rows, pair_rows), 1)
    b_sees_a = jnp.logical_and(r_id >= GLA_CHUNK, c_id < GLA_CHUNK)
    in_b = lax.broadcasted_iota(jnp.int32, (pair_rows, 1), 0) >= GLA_CHUNK
    q_scale = dk ** -0.5

    def pair(p, carry):
        rows = pl.ds(pl.multiple_of(p * pair_rows, pair_rows), pair_rows)
        y = _dot(gl_ref[rows, :].astype(BF16), wgu_ref[...]) + bg_ref[...]
        log_alpha = (jnp.minimum(y, 0.0) - jnp.log(1.0 + jnp.exp(-jnp.abs(y)))) * (1.0 / GLA_TAU)
        hi, lo = _split_bf16(log_alpha)
        b_cum = _dot(tril, hi) + _dot(tril, lo)
        b_last_a = b_cum[GLA_CHUNK - 1:GLA_CHUNK, :]
        b_last_b = b_cum[pair_rows - 1:pair_rows, :]
        d_a = jnp.exp(b_last_a)
        d_b = jnp.exp(b_last_b)
        e_pos = jnp.exp(b_cum)
        e_neg = jnp.exp(-b_cum)
        e_end = jnp.exp(jnp.where(in_b, b_last_b, b_last_a) - b_cum)
        e_pos_s = e_pos * jnp.where(in_b, d_a, 1.0)
        e_end_s = e_end * jnp.where(in_b, 1.0, d_b)
        decay_rows = jnp.broadcast_to(d_a * d_b, (V7X_LANES, kw))
        for h in range(GLA_HEADS):
            ksl = slice(h * dk, (h + 1) * dk)
            vsl = slice(h * dv, (h + 1) * dv)
            q = qk_ref[rows, ksl] * q_scale
            k = qk_ref[rows, kw + h * dk:kw + (h + 1) * dk]
            q_dec = (q * e_pos[:, ksl]).astype(BF16)
            k_dec = (k * e_neg[:, ksl]).astype(BF16)
            k_end = (k * e_end[:, ksl]).astype(BF16)
            q_dec_s = (q * e_pos_s[:, ksl]).astype(BF16)
            k_end_s = (k * e_end_s[:, ksl]).astype(BF16)
            v = v_ref[rows, vsl]
            scores = jnp.where(own_causal, _dot_nt(q_dec, k_dec),
                               jnp.where(b_sees_a, _dot_nt(q_dec, k_end), 0.0)).astype(BF16)
            state = state_ref[h]
            o = _dot(scores, v) + _dot(q_dec_s, state.astype(BF16))
            decay = jnp.concatenate(
                [decay_rows[:, h * dk + b * V7X_LANES:h * dk + (b + 1) * V7X_LANES].T
                 for b in range(dk // V7X_LANES)], axis=0)
            decay = jnp.concatenate([decay] * (dv // V7X_LANES), axis=1)
            state_ref[h] = state * decay + _dot_tn(k_end_s, v)
            ms = jnp.mean(o * o, axis=-1, keepdims=True)
            o = o * lax.rsqrt(ms + EPS) * ng_ref[...]
            o_ref[rows, vsl] = (_silu(r_ref[rows, vsl]) * o).astype(o_ref.dtype)
        return carry

    lax.fori_loop(0, pairs, pair, 0)


def _gla(qk, v, r, g_low, w_gate_up, b_gate, norm_g):
    s = qk.shape[0]
    kw = w_gate_up.shape[1]
    vw = v.shape[1]
    dk, dv = kw // GLA_HEADS, vw // GLA_HEADS
    rows = _tile(s, 256)
    pair_rows = 2 * GLA_CHUNK
    pairs = rows // pair_rows
    t = jnp.arange(pair_rows)
    tril = jnp.logical_and(t[:, None] >= t[None, :],
                           t[:, None] // GLA_CHUNK == t[None, :] // GLA_CHUNK).astype(BF16)
    return pl.pallas_call(
        functools.partial(_gla_kernel, pairs=pairs, dk=dk, dv=dv),
        grid=(s // rows,),
        in_specs=[pl.BlockSpec((rows, 2 * kw), lambda i: (i, 0)),
                  pl.BlockSpec((rows, vw), lambda i: (i, 0)),
                  pl.BlockSpec((rows, vw), lambda i: (i, 0)),
                  pl.BlockSpec((rows, V7X_LANES), lambda i: (i, 0)),
                  pl.BlockSpec((V7X_LANES, kw), lambda i: (0, 0)),
                  pl.BlockSpec((1, kw), lambda i: (0, 0)),
                  pl.BlockSpec((1, dv), lambda i: (0, 0)),
                  pl.BlockSpec((pair_rows, pair_rows), lambda i: (0, 0))],
        out_specs=pl.BlockSpec((rows, vw), lambda i: (i, 0)),
        out_shape=jax.ShapeDtypeStruct((s, vw), BF16),
        scratch_shapes=[pltpu.VMEM((GLA_HEADS, dk, dv), F32)],
        compiler_params=_params("arbitrary"),
        name="gla",
    )(qk, v, r, g_low, w_gate_up, b_gate.reshape(1, kw), norm_g.reshape(1, dv), tril)


def _sb_kernel(q_ref, k_ref, v_ref, t_ref, o_ref, acc_ref, run_ref, *, tq, tk, heads):
    qi = pl.program_id(1)
    tmat = t_ref[...]
    row_groups = tq // tk
    base = qi * row_groups
    below_diag = (lax.broadcasted_iota(jnp.int32, (tk, tk), 1)
                  < lax.broadcasted_iota(jnp.int32, (tk, tk), 0))

    chains = heads * row_groups
    all_chains = range(chains)
    group = [c % row_groups for c in all_chains]
    rows = [slice(g * tk, (g + 1) * tk) for g in group]
    lanes = [slice((c // row_groups) * SB_HEAD_DIM, (c // row_groups + 1) * SB_HEAD_DIM)
             for c in all_chains]

    def sweep(blocks, diagonal, live):
        starts = [pl.multiple_of(blocks[g] * tk, tk) for g in group]
        if live is not None:
            live = [live[g] for g in group]
        zs = [_dot_nt(q_ref[rows[c], lanes[c]], k_ref[pl.ds(starts[c], tk), lanes[c]])
              for c in all_chains]
        log_betas, halves = [], []
        for z in zs:
            softplus = jnp.maximum(z, 0.0) + jnp.log(1.0 + jnp.exp(-jnp.abs(z)))
            log_betas.append(z - softplus)
            neg_log_rest = softplus
            if diagonal:
                neg_log_rest = jnp.where(below_diag, neg_log_rest, 0.0)
            halves.append(jnp.concatenate(_split_bf16(neg_log_rest), axis=1))
        cms = [_dot(hl, tmat) for hl in halves]
        atts, runs = [], []
        for c in all_chains:
            log_att = log_betas[c] + cms[c][:, :tk]
            if diagonal:
                atts.append(jnp.where(below_diag, jnp.exp(log_att), 0.0))
                runs.append(cms[c][:, tk:])
            else:
                run = jnp.where(live[c], run_ref[rows[c], lanes[c]], -1e30)
                atts.append(jnp.exp(log_att + run))
                runs.append(run + cms[c][:, tk:])
        pvs = [_dot(atts[c].astype(BF16), v_ref[pl.ds(starts[c], tk), lanes[c]])
               for c in all_chains]
        if not diagonal:
            pvs = [acc_ref[rows[c], lanes[c]] + pvs[c] for c in all_chains]
        for c in all_chains:
            acc_ref[rows[c], lanes[c]] = pvs[c]
            run_ref[rows[c], lanes[c]] = runs[c]
        return jnp.max(functools.reduce(jnp.maximum, runs))

    top = sweep([base + g for g in range(row_groups)], True, None)

    def cond(st):
        n, top = st
        return jnp.logical_and(n <= base + row_groups - 1, top > F32_EXP_UNDERFLOW)

    def body(st):
        n, _ = st
        blocks = [base + g - n for g in range(row_groups)]
        top = sweep([jnp.maximum(j, 0) for j in blocks], False, [j >= 0 for j in blocks])
        return n + 1, top

    lax.while_loop(cond, body, (jnp.int32(1), top))
    o_ref[...] = acc_ref[...].astype(o_ref.dtype)


def _sb_attention(q, k, v):
    s, d = q.shape
    dh = SB_HEAD_DIM
    tk = V7X_LANES
    tq = _tile(s, 1024)
    r = lax.broadcasted_iota(jnp.int32, (tk, 2 * tk), 0)
    c = lax.broadcasted_iota(jnp.int32, (tk, 2 * tk), 1)
    half = jnp.where(jnp.logical_or(c >= tk, r > c), -1.0, 0.0).astype(BF16)
    tmat = jnp.concatenate([half, half], axis=0)
    heads = SB_HEADS_PER_STEP
    wh = heads * dh
    return pl.pallas_call(
        functools.partial(_sb_kernel, tq=tq, tk=tk, heads=heads),
        grid=(d // wh, s // tq),
        in_specs=[pl.BlockSpec((tq, wh), lambda h, i: (i, h)),
                  pl.BlockSpec((s, wh), lambda h, i: (0, h)),
                  pl.BlockSpec((s, wh), lambda h, i: (0, h)),
                  pl.BlockSpec((2 * tk, 2 * tk), lambda h, i: (0, 0))],
        out_specs=pl.BlockSpec((tq, wh), lambda h, i: (i, h)),
        out_shape=jax.ShapeDtypeStruct((s, d), BF16),
        scratch_shapes=[pltpu.VMEM((tq, wh), F32), pltpu.VMEM((tq, wh), F32)],
        compiler_params=_params("arbitrary", "arbitrary"),
        name="sb_attention",
    )(q, k, v, tmat)


def _ffn_block(x, mod, l, ffn_w_in, ffn_conv_w, ffn_conv_b, ffn_w_out_bf):
    d = x.shape[1]
    shift, scale, gate = (mod[i * d:(i + 1) * d] for i in (3, 4, 5))
    h = _norm_affine(x, scale, shift, True)
    g = _ffn_in(h, ffn_w_in, ffn_conv_w, ffn_conv_b, l)
    return _mm_residual(g, ffn_w_out_bf, x, gate, layer=l)


def kernel(x, c, w_ada, b_ada, ada_table, gla_w_in, gla_w_gate_up, gla_b_gate, gla_norm_g, gla_w_out, kv_norm_g, w_kv, k_norm_g, sb_w_q, sb_q_norm_g, sb_w_out, ffn_w_in, ffn_conv_w, ffn_conv_b, ffn_w_out):
    batch, s, d = x.shape
    assert batch == 1, "adaLN modulation rows are built for a single sequence"
    depth = ada_table.shape[0]
    n_a = gla_w_in.shape[0]
    kw = gla_w_gate_up.shape[2]
    rank = gla_w_gate_up.shape[1]
    x = x.reshape(s, d)

    gla_w_in_bf = gla_w_in.astype(BF16)
    gla_w_out_bf = gla_w_out.astype(BF16)
    w_kv_bf = w_kv.astype(BF16)
    sb_w_q_bf = sb_w_q.astype(BF16)
    sb_w_out_bf = sb_w_out.astype(BF16)
    ffn_w_out_bf = ffn_w_out.astype(BF16)
    w_up_bf = jnp.pad(gla_w_gate_up, ((0, 0), (0, V7X_LANES - rank), (0, 0))).astype(BF16)

    mods = _cond(c, w_ada, b_ada, ada_table)
    k_sh = v_sh = None
    for l in range(depth):
        mod = mods[l]
        shift, scale, gate = (mod[i * d:(i + 1) * d] for i in (0, 1, 2))
        h = _norm_affine(x, scale, shift, True)
        if l < n_a:
            qk = _mm_plain(h, gla_w_in_bf, F32, layer=l, col0=0, n=2 * kw)
            v = _mm_plain(h, gla_w_in_bf, BF16, layer=l, col0=2 * kw, n=d)
            r = _mm_plain(h, gla_w_in_bf, F32, layer=l, col0=2 * kw + d, n=d)
            g_low = _mm_plain(h, gla_w_in_bf, F32, layer=l, col0=2 * kw + 2 * d,
                              n=V7X_LANES, valid_cols=rank)
            o = _gla(qk, v, r, g_low, w_up_bf[l], gla_b_gate[l], gla_norm_g[l])
            x = _mm_residual(o, gla_w_out_bf, x, gate, layer=l)
        else:
            if l == n_a:
                src = _norm_affine(x, kv_norm_g, jnp.zeros_like(kv_norm_g), False)
                k_sh = _mm_headnorm(src, w_kv_bf, k_norm_g, 1.0, col0=0, n=d)
                v_sh = _mm_plain(src, w_kv_bf, BF16, col0=d, n=d)
            j = l - n_a
            q = _mm_headnorm(h, sb_w_q_bf, sb_q_norm_g[j], SB_HEAD_DIM ** -0.5, layer=j)
            o = _sb_attention(q, k_sh, v_sh)
            x = _mm_residual(o, sb_w_out_bf, x, gate, layer=j)
        x = _ffn_block(x, mod, l, ffn_w_in, ffn_conv_w, ffn_conv_b, ffn_w_out_bf)
    return x.reshape(batch, s, d)
```

```python
import functools

import jax
import jax.numpy as jnp
from jax import lax
from jax.experimental import pallas as pl
from jax.experimental.pallas import tpu as pltpu

F32 = jnp.float32
BF16 = jnp.bfloat16

EPS = 1e-6
N_MOD = 6
CONV_WIDTH = 3
GLA_CHUNK = 64
GLA_HEADS = 4
GLA_TAU = 16.0
SB_HEAD_DIM = 128
SB_HEADS_PER_STEP = 2

V7X_LANES = 128
V7X_SUBLANES = 8
V7X_VMEM_LIMIT_BYTES = 56 * 1024 * 1024

F32_EXP_UNDERFLOW = -88.0


def _params(*semantics):
    return pltpu.CompilerParams(dimension_semantics=semantics,
                                vmem_limit_bytes=V7X_VMEM_LIMIT_BYTES)


def _tile(n, pref):
    if n <= pref:
        return n
    t = (pref // V7X_LANES) * V7X_LANES
    while t >= V7X_LANES:
        if n % t == 0:
            return t
        t -= V7X_LANES
    return n


def _dot(a, b):
    return jnp.dot(a, b, preferred_element_type=F32)


def _dot_nt(a, b):
    return lax.dot_general(a, b, (((1,), (1,)), ((), ())), preferred_element_type=F32)


def _dot_tn(a, b):
    return lax.dot_general(a, b, (((0,), (0,)), ((), ())), preferred_element_type=F32)


def _silu(x):
    return x / (1.0 + jnp.exp(-x))


def _split_bf16(x):
    hi = x.astype(BF16)
    lo = (x - hi.astype(F32)).astype(BF16)
    return hi, lo


def _cond_kernel(c_ref, w_ref, b_ref, tab_ref, o_ref):
    s = _silu(c_ref[...]).astype(BF16)
    acc = _dot(s, w_ref[...].astype(BF16))
    o_ref[...] = acc[0:1, :] + b_ref[...] + tab_ref[...]


def _cond(c, w_ada, b_ada, ada_table):
    d = c.shape[1]
    depth = ada_table.shape[0]
    n = w_ada.shape[1]
    tn = _tile(n, 512)
    c8 = jnp.broadcast_to(c, (V7X_SUBLANES, d))
    return pl.pallas_call(
        _cond_kernel,
        grid=(n // tn,),
        in_specs=[pl.BlockSpec((V7X_SUBLANES, d), lambda j: (0, 0)),
                  pl.BlockSpec((d, tn), lambda j: (0, j)),
                  pl.BlockSpec((1, tn), lambda j: (0, j)),
                  pl.BlockSpec((depth, tn), lambda j: (0, j))],
        out_specs=pl.BlockSpec((depth, tn), lambda j: (0, j)),
        out_shape=jax.ShapeDtypeStruct((depth, n), F32),
        compiler_params=_params("arbitrary"),
        name="cond",
    )(c8, w_ada, b_ada.reshape(1, n), ada_table.reshape(depth, n))


def _norm_affine_kernel(x_ref, mul_ref, add_ref, o_ref, *, plus_one):
    x = x_ref[...]
    ms = jnp.mean(x * x, axis=-1, keepdims=True)
    xn = x * lax.rsqrt(ms + EPS)
    mul = mul_ref[...]
    if plus_one:
        mul = 1.0 + mul
    o_ref[...] = (xn * mul + add_ref[...]).astype(o_ref.dtype)


def _norm_affine(x, mul, add, plus_one):
    s, d = x.shape
    tm = _tile(s, 256)
    return pl.pallas_call(
        functools.partial(_norm_affine_kernel, plus_one=plus_one),
        grid=(s // tm,),
        in_specs=[pl.BlockSpec((tm, d), lambda i: (i, 0)),
                  pl.BlockSpec((1, d), lambda i: (0, 0)),
                  pl.BlockSpec((1, d), lambda i: (0, 0))],
        out_specs=pl.BlockSpec((tm, d), lambda i: (i, 0)),
        out_shape=jax.ShapeDtypeStruct((s, d), BF16),
        compiler_params=_params("arbitrary"),
        name="norm_affine",
    )(x, mul.reshape(1, d), add.reshape(1, d))


def _mm_plain_kernel(a_ref, w_ref, o_ref, *, valid_cols=None):
    w = w_ref[...]
    if valid_cols is not None:
        col = lax.broadcasted_iota(jnp.int32, w.shape, 1)
        w = jnp.where(col < valid_cols, w, jnp.zeros_like(w))
    o_ref[...] = _dot(a_ref[...], w).astype(o_ref.dtype)


def _mm_headnorm_kernel(a_ref, w_ref, g_ref, o_ref, *, post_scale):
    acc = _dot(a_ref[...], w_ref[...])
    g = g_ref[...] * post_scale
    for c in range(acc.shape[1] // SB_HEAD_DIM):
        sl = slice(c * SB_HEAD_DIM, (c + 1) * SB_HEAD_DIM)
        blk = acc[:, sl]
        ms = jnp.mean(blk * blk, axis=-1, keepdims=True)
        o_ref[:, sl] = (blk * lax.rsqrt(ms + EPS) * g).astype(o_ref.dtype)


def _mm_residual_kernel(a_ref, w_ref, x_ref, gate_ref, o_ref):
    o_ref[...] = x_ref[...] + gate_ref[...] * _dot(a_ref[...], w_ref[...])


def _mm_tiles(m, n, k):
    tm = _tile(m, 1024)
    tn = _tile(n, 1024 if k <= 4096 else 512)
    if k > 4096:
        tm = _tile(m, 512)
    return tm, tn


def _weight_spec(w, layer, col0, tn):
    k = w.shape[-2]
    assert col0 % tn == 0
    c0 = col0 // tn
    if w.ndim == 2:
        return pl.BlockSpec((k, tn), lambda i, j: (0, c0 + j))
    return pl.BlockSpec((None, k, tn), lambda i, j: (layer, 0, c0 + j))


def _mm_plain(a, w, out_dtype, *, layer=0, col0=0, n=None, valid_cols=None):
    m, k = a.shape
    n = w.shape[-1] if n is None else n
    tm, tn = _mm_tiles(m, n, k)
    return pl.pallas_call(
        functools.partial(_mm_plain_kernel, valid_cols=valid_cols),
        grid=(m // tm, n // tn),
        in_specs=[pl.BlockSpec((tm, k), lambda i, j: (i, 0)),
                  _weight_spec(w, layer, col0, tn)],
        out_specs=pl.BlockSpec((tm, tn), lambda i, j: (i, j)),
        out_shape=jax.ShapeDtypeStruct((m, n), out_dtype),
        compiler_params=_params("arbitrary", "arbitrary"),
        name="mm_plain",
    )(a, w)


def _mm_headnorm(a, w, g, post_scale, *, layer=0, col0=0, n=None):
    m, k = a.shape
    n = w.shape[-1] if n is None else n
    tm, tn = _mm_tiles(m, n, k)
    return pl.pallas_call(
        functools.partial(_mm_headnorm_kernel, post_scale=post_scale),
        grid=(m // tm, n // tn),
        in_specs=[pl.BlockSpec((tm, k), lambda i, j: (i, 0)),
                  _weight_spec(w, layer, col0, tn),
                  pl.BlockSpec((1, SB_HEAD_DIM), lambda i, j: (0, 0))],
        out_specs=pl.BlockSpec((tm, tn), lambda i, j: (i, j)),
        out_shape=jax.ShapeDtypeStruct((m, n), BF16),
        compiler_params=_params("arbitrary", "arbitrary"),
        name="mm_headnorm",
    )(a, w, g.reshape(1, SB_HEAD_DIM))


def _mm_residual(a, w, x, gate, *, layer=0):
    m, k = a.shape
    n = w.shape[-1]
    tm, tn = _mm_tiles(m, n, k)
    return pl.pallas_call(
        _mm_residual_kernel,
        grid=(m // tm, n // tn),
        in_specs=[pl.BlockSpec((tm, k), lambda i, j: (i, 0)),
                  _weight_spec(w, layer, 0, tn),
                  pl.BlockSpec((tm, tn), lambda i, j: (i, j)),
                  pl.BlockSpec((1, tn), lambda i, j: (0, j))],
        out_specs=pl.BlockSpec((tm, tn), lambda i, j: (i, j)),
        out_shape=jax.ShapeDtypeStruct((m, n), F32),
        compiler_params=_params("arbitrary", "arbitrary"),
        name="mm_residual",
    )(a, w, x, gate.reshape(1, n))


FFN_CHUNK_ROWS = 128


def _ffn_in_kernel(h_ref, wa_ref, wb_ref, cwa_ref, cwb_ref, cba_ref, cbb_ref, w2_ref,
                   o_ref, w2_bf_ref, wa_bf_ref, wb_bf_ref, carry_a_ref, carry_b_ref,
                   ua_ref, ub_ref):
    tm = h_ref.shape[0]
    cm = min(FFN_CHUNK_ROWS, tm)

    w2_bf_ref[...] = w2_ref[...].astype(BF16)

    @pl.when(pl.program_id(1) == 0)
    def _():
        wa_bf_ref[...] = wa_ref[...].astype(BF16)
        wb_bf_ref[...] = wb_ref[...].astype(BF16)
        carry_a_ref[...] = jnp.zeros_like(carry_a_ref)
        carry_b_ref[...] = jnp.zeros_like(carry_b_ref)

    row = lax.broadcasted_iota(jnp.int32, carry_a_ref.shape, 0)

    def conv(u, prev, cw_ref, cb_ref):
        w0, w1, w2 = cw_ref[0:1, :], cw_ref[1:2, :], cw_ref[2:3, :]
        bias = cb_ref[...]
        u1 = pltpu.roll(u, 1, axis=0)
        u2 = pltpu.roll(u, 2, axis=0)
        head1 = jnp.where(row < 1, pltpu.roll(prev, 1, axis=0), u1[0:V7X_SUBLANES])
        head2 = jnp.where(row < 2, pltpu.roll(prev, 2, axis=0), u2[0:V7X_SUBLANES])
        head = bias + head2 * w0 + head1 * w1 + u[0:V7X_SUBLANES] * w2
        tail = (bias + u2[V7X_SUBLANES:] * w0 + u1[V7X_SUBLANES:] * w1
                + u[V7X_SUBLANES:] * w2)
        return head, tail

    prev_a = carry_a_ref[...]
    prev_b = carry_b_ref[...]
    for c in range(tm // cm):
        r0 = c * cm
        h = h_ref[r0:r0 + cm, :]
        slot = lax.rem(pl.program_id(1) + c, 2)
        ua_ref[slot] = _dot(h, wa_bf_ref[...])
        ub_ref[slot] = _dot(h, wb_bf_ref[...])
        ua = ua_ref[slot]
        ub = ub_ref[slot]
        a_head, a_tail = conv(ua, prev_a, cwa_ref, cba_ref)
        b_head, b_tail = conv(ub, prev_b, cwb_ref, cbb_ref)
        o_ref[r0:r0 + V7X_SUBLANES, :] = (_silu(a_head) * b_head).astype(o_ref.dtype)
        o_ref[r0 + V7X_SUBLANES:r0 + cm, :] = (_silu(a_tail) * b_tail).astype(o_ref.dtype)
        prev_a = ua[cm - V7X_SUBLANES:cm]
        prev_b = ub[cm - V7X_SUBLANES:cm]
    carry_a_ref[...] = prev_a
    carry_b_ref[...] = prev_b


def _ffn_in(h, w_in, conv_w, conv_b, w_out, layer):
    s, d = h.shape
    f = w_in.shape[-1] // 2
    tm = _tile(s, 1024)
    tn = _tile(f, 256)
    nj = f // tn
    ni = s // tm
    cm = min(FFN_CHUNK_ROWS, tm)
    slab = f // (nj * ni)
    assert slab * nj * ni == f and slab % (2 * V7X_SUBLANES) == 0, (f, nj, ni)
    conv_b = conv_b.reshape(conv_b.shape[0], 1, 2 * f)
    return pl.pallas_call(
        _ffn_in_kernel,
        grid=(nj, ni),
        in_specs=[pl.BlockSpec((tm, d), lambda j, i: (i, 0)),
                  pl.BlockSpec((None, d, tn), lambda j, i: (layer, 0, j)),
                  pl.BlockSpec((None, d, tn), lambda j, i: (layer, 0, j + nj)),
                  pl.BlockSpec((None, CONV_WIDTH, tn), lambda j, i: (layer, 0, j)),
                  pl.BlockSpec((None, CONV_WIDTH, tn), lambda j, i: (layer, 0, j + nj)),
                  pl.BlockSpec((None, 1, tn), lambda j, i: (layer, 0, j)),
                  pl.BlockSpec((None, 1, tn), lambda j, i: (layer, 0, j + nj)),
                  pl.BlockSpec((None, slab, d), lambda j, i: (layer, j * ni + i, 0))],
        out_specs=[pl.BlockSpec((tm, tn), lambda j, i: (i, j)),
                   pl.BlockSpec((slab, d), lambda j, i: (j * ni + i, 0))],
        out_shape=[jax.ShapeDtypeStruct((s, f), BF16),
                   jax.ShapeDtypeStruct((f, d), BF16)],
        scratch_shapes=[pltpu.VMEM((d, tn), BF16),
                        pltpu.VMEM((d, tn), BF16),
                        pltpu.VMEM((V7X_SUBLANES, tn), F32),
                        pltpu.VMEM((V7X_SUBLANES, tn), F32),
                        pltpu.VMEM((2, cm, tn), F32),
                        pltpu.VMEM((2, cm, tn), F32)],
        compiler_params=_params("arbitrary", "arbitrary"),
        name="ffn_in",
    )(h, w_in, w_in, conv_w, conv_w, conv_b, conv_b, w_out)


def _gla_kernel(qk_ref, v_ref, r_ref, gl_ref, wgu_ref, bg_ref, ng_ref, tril_ref,
                o_ref, state_ref, *, pairs, dk, dv):
    kw = GLA_HEADS * dk
    pair_rows = 2 * GLA_CHUNK

    @pl.when(pl.program_id(0) == 0)
    def _():
        state_ref[...] = jnp.zeros_like(state_ref)

    tril = tril_ref[...]
    own_causal = tril > 0
    r_id = lax.broadcasted_iota(jnp.int32, (pair_rows, pair_rows), 0)
    c_id = lax.broadcasted_iota(jnp.int32, (pair_rows, pair_rows), 1)
    b_sees_a = jnp.logical_and(r_id >= GLA_CHUNK, c_id < GLA_CHUNK)
    in_b = lax.broadcasted_iota(jnp.int32, (pair_rows, 1), 0) >= GLA_CHUNK
    q_scale = dk ** -0.5

    def pair(p, carry):
        rows = pl.ds(pl.multiple_of(p * pair_rows, pair_rows), pair_rows)
        y = _dot(gl_ref[rows, :].astype(BF16), wgu_ref[...]) + bg_ref[...]
        log_alpha = (jnp.minimum(y, 0.0) - jnp.log(1.0 + jnp.exp(-jnp.abs(y)))) * (1.0 / GLA_TAU)
        hi, lo = _split_bf16(log_alpha)
        b_cum = _dot(tril, hi) + _dot(tril, lo)
        b_last_a = b_cum[GLA_CHUNK - 1:GLA_CHUNK, :]
        b_last_b = b_cum[pair_rows - 1:pair_rows, :]
        d_a = jnp.exp(b_last_a)
        d_b = jnp.exp(b_last_b)
        e_pos = jnp.exp(b_cum)
        e_neg = jnp.exp(-b_cum)
        e_end = jnp.exp(jnp.where(in_b, b_last_b, b_last_a) - b_cum)
        e_pos_s = e_pos * jnp.where(in_b, d_a, 1.0)
        e_end_s = e_end * jnp.where(in_b, 1.0, d_b)
        decay_rows = jnp.broadcast_to(d_a * d_b, (V7X_LANES, kw))
        for h in range(GLA_HEADS):
            ksl = slice(h * dk, (h + 1) * dk)
            vsl = slice(h * dv, (h + 1) * dv)
            q = qk_ref[rows, ksl] * q_scale
            k = qk_ref[rows, kw + h * dk:kw + (h + 1) * dk]
            q_dec = (q * e_pos[:, ksl]).astype(BF16)
            k_dec = (k * e_neg[:, ksl]).astype(BF16)
            k_end = (k * e_end[:, ksl]).astype(BF16)
            q_dec_s = (q * e_pos_s[:, ksl]).astype(BF16)
            k_end_s = (k * e_end_s[:, ksl]).astype(BF16)
            v = v_ref[rows, vsl]
            scores = jnp.where(own_causal, _dot_nt(q_dec, k_dec),
                               jnp.where(b_sees_a, _dot_nt(q_dec, k_end), 0.0)).astype(BF16)
            state = state_ref[h]
            o = _dot(scores, v) + _dot(q_dec_s, state.astype(BF16))
            decay = jnp.concatenate(
                [decay_rows[:, h * dk + b * V7X_LANES:h * dk + (b + 1) * V7X_LANES].T
                 for b in range(dk // V7X_LANES)], axis=0)
            decay = jnp.concatenate([decay] * (dv // V7X_LANES), axis=1)
            state_ref[h] = state * decay + _dot_tn(k_end_s, v)
            ms = jnp.mean(o * o, axis=-1, keepdims=True)
            o = o * lax.rsqrt(ms + EPS) * ng_ref[...]
            o_ref[rows, vsl] = (_silu(r_ref[rows, vsl]) * o).astype(o_ref.dtype)
        return carry

    lax.fori_loop(0, pairs, pair, 0)


def _gla(qk, v, r, g_low, w_gate_up, b_gate, norm_g):
    s = qk.shape[0]
    kw = w_gate_up.shape[1]
    vw = v.shape[1]
    dk, dv = kw // GLA_HEADS, vw // GLA_HEADS
    rows = _tile(s, 256)
    pair_rows = 2 * GLA_CHUNK
    pairs = rows // pair_rows
    t = jnp.arange(pair_rows)
    tril = jnp.logical_and(t[:, None] >= t[None, :],
                           t[:, None] // GLA_CHUNK == t[None, :] // GLA_CHUNK).astype(BF16)
    return pl.pallas_call(
        functools.partial(_gla_kernel, pairs=pairs, dk=dk, dv=dv),
        grid=(s // rows,),
        in_specs=[pl.BlockSpec((rows, 2 * kw), lambda i: (i, 0)),
                  pl.BlockSpec((rows, vw), lambda i: (i, 0)),
                  pl.BlockSpec((rows, vw), lambda i: (i, 0)),
                  pl.BlockSpec((rows, V7X_LANES), lambda i: (i, 0)),
                  pl.BlockSpec((V7X_LANES, kw), lambda i: (0, 0)),
                  pl.BlockSpec((1, kw), lambda i: (0, 0)),
                  pl.BlockSpec((1, dv), lambda i: (0, 0)),
                  pl.BlockSpec((pair_rows, pair_rows), lambda i: (0, 0))],
        out_specs=pl.BlockSpec((rows, vw), lambda i: (i, 0)),
        out_shape=jax.ShapeDtypeStruct((s, vw), BF16),
        scratch_shapes=[pltpu.VMEM((GLA_HEADS, dk, dv), F32)],
        compiler_params=_params("arbitrary"),
        name="gla",
    )(qk, v, r, g_low, w_gate_up, b_gate.reshape(1, kw), norm_g.reshape(1, dv), tril)


def _sb_kernel(q_ref, k_ref, v_ref, t_ref, o_ref, acc_ref, run_ref, *, tq, tk, heads):
    qi = pl.program_id(1)
    tmat = t_ref[...]
    row_groups = tq // tk
    base = qi * row_groups
    below_diag = (lax.broadcasted_iota(jnp.int32, (tk, tk), 1)
                  < lax.broadcasted_iota(jnp.int32, (tk, tk), 0))

    chains = heads * row_groups
    all_chains = range(chains)
    group = [c % row_groups for c in all_chains]
    rows = [slice(g * tk, (g + 1) * tk) for g in group]
    lanes = [slice((c // row_groups) * SB_HEAD_DIM, (c // row_groups + 1) * SB_HEAD_DIM)
             for c in all_chains]

    def sweep(blocks, diagonal, live):
        starts = [pl.multiple_of(blocks[g] * tk, tk) for g in group]
        if live is not None:
            live = [live[g] for g in group]
        zs = [_dot_nt(q_ref[rows[c], lanes[c]], k_ref[pl.ds(starts[c], tk), lanes[c]])
              for c in all_chains]
        log_betas, halves = [], []
        for z in zs:
            softplus = jnp.maximum(z, 0.0) + jnp.log(1.0 + jnp.exp(-jnp.abs(z)))
            log_betas.append(z - softplus)
            neg_log_rest = softplus
            if diagonal:
                neg_log_rest = jnp.where(below_diag, neg_log_rest, 0.0)
            halves.append(jnp.concatenate(_split_bf16(neg_log_rest), axis=1))
        cms = [_dot(hl, tmat) for hl in halves]
        atts, runs = [], []
        for c in all_chains:
            log_att = log_betas[c] + cms[c][:, :tk]
            if diagonal:
                atts.append(jnp.where(below_diag, jnp.exp(log_att), 0.0))
                runs.append(cms[c][:, tk:])
            else:
                run = jnp.where(live[c], run_ref[rows[c], lanes[c]], -1e30)
                atts.append(jnp.exp(log_att + run))
                runs.append(run + cms[c][:, tk:])
        pvs = [_dot(atts[c].astype(BF16), v_ref[pl.ds(starts[c], tk), lanes[c]])
               for c in all_chains]
        if not diagonal:
            pvs = [acc_ref[rows[c], lanes[c]] + pvs[c] for c in all_chains]
        for c in all_chains:
            acc_ref[rows[c], lanes[c]] = pvs[c]
            run_ref[rows[c], lanes[c]] = runs[c]
        return jnp.max(functools.reduce(jnp.maximum, runs))

    top = sweep([base + g for g in range(row_groups)], True, None)

    def cond(st):
        n, top = st
        return jnp.logical_and(n <= base + row_groups - 1, top > F32_EXP_UNDERFLOW)

    def body(st):
        n, _ = st
        blocks = [base + g - n for g in range(row_groups)]
        top = sweep([jnp.maximum(j, 0) for j in blocks], False, [j >= 0 for j in blocks])
        return n + 1, top

    lax.while_loop(cond, body, (jnp.int32(1), top))
    o_ref[...] = acc_ref[...].astype(o_ref.dtype)


def _sb_attention(q, k, v):
    s, d = q.shape
    dh = SB_HEAD_DIM
    tk = V7X_LANES
    tq = _tile(s, 1024)
    r = lax.broadcasted_iota(jnp.int32, (tk, 2 * tk), 0)
    c = lax.broadcasted_iota(jnp.int32, (tk, 2 * tk), 1)
    half = jnp.where(jnp.logical_or(c >= tk, r > c), -1.0, 0.0).astype(BF16)
    tmat = jnp.concatenate([half, half], axis=0)
    heads = SB_HEADS_PER_STEP
    wh = heads * dh
    return pl.pallas_call(
        functools.partial(_sb_kernel, tq=tq, tk=tk, heads=heads),
        grid=(d // wh, s // tq),
        in_specs=[pl.BlockSpec((tq, wh), lambda h, i: (i, h)),
                  pl.BlockSpec((s, wh), lambda h, i: (0, h)),
                  pl.BlockSpec((s, wh), lambda h, i: (0, h)),
                  pl.BlockSpec((2 * tk, 2 * tk), lambda h, i: (0, 0))],
        out_specs=pl.BlockSpec((tq, wh), lambda h, i: (i, h)),
        out_shape=jax.ShapeDtypeStruct((s, d), BF16),
        scratch_shapes=[pltpu.VMEM((tq, wh), F32), pltpu.VMEM((tq, wh), F32)],
        compiler_params=_params("arbitrary", "arbitrary"),
        name="sb_attention",
    )(q, k, v, tmat)


def _ffn_block(x, mod, l, ffn_w_in, ffn_conv_w, ffn_conv_b, ffn_w_out):
    d = x.shape[1]
    shift, scale, gate = (mod[i * d:(i + 1) * d] for i in (3, 4, 5))
    h = _norm_affine(x, scale, shift, True)
    g, w_out_bf = _ffn_in(h, ffn_w_in, ffn_conv_w, ffn_conv_b, ffn_w_out, l)
    return _mm_residual(g, w_out_bf, x, gate)


def kernel(x, c, w_ada, b_ada, ada_table, gla_w_in, gla_w_gate_up, gla_b_gate, gla_norm_g, gla_w_out, kv_norm_g, w_kv, k_norm_g, sb_w_q, sb_q_norm_g, sb_w_out, ffn_w_in, ffn_conv_w, ffn_conv_b, ffn_w_out):
    batch, s, d = x.shape
    assert batch == 1, "adaLN modulation rows are built for a single sequence"
    depth = ada_table.shape[0]
    n_a = gla_w_in.shape[0]
    kw = gla_w_gate_up.shape[2]
    rank = gla_w_gate_up.shape[1]
    x = x.reshape(s, d)

    gla_w_in_bf = gla_w_in.astype(BF16)
    gla_w_out_bf = gla_w_out.astype(BF16)
    w_kv_bf = w_kv.astype(BF16)
    sb_w_q_bf = sb_w_q.astype(BF16)
    sb_w_out_bf = sb_w_out.astype(BF16)
    w_up_bf = jnp.pad(gla_w_gate_up, ((0, 0), (0, V7X_LANES - rank), (0, 0))).astype(BF16)

    mods = _cond(c, w_ada, b_ada, ada_table)
    k_sh = v_sh = None
    for l in range(depth):
        mod = mods[l]
        shift, scale, gate = (mod[i * d:(i + 1) * d] for i in (0, 1, 2))
        h = _norm_affine(x, scale, shift, True)
        if l < n_a:
            qk = _mm_plain(h, gla_w_in_bf, F32, layer=l, col0=0, n=2 * kw)
            v = _mm_plain(h, gla_w_in_bf, BF16, layer=l, col0=2 * kw, n=d)
            r = _mm_plain(h, gla_w_in_bf, F32, layer=l, col0=2 * kw + d, n=d)
            g_low = _mm_plain(h, gla_w_in_bf, F32, layer=l, col0=2 * kw + 2 * d,
                              n=V7X_LANES, valid_cols=rank)
            o = _gla(qk, v, r, g_low, w_up_bf[l], gla_b_gate[l], gla_norm_g[l])
            x = _mm_residual(o, gla_w_out_bf, x, gate, layer=l)
        else:
            if l == n_a:
                src = _norm_affine(x, kv_norm_g, jnp.zeros_like(kv_norm_g), False)
                k_sh = _mm_headnorm(src, w_kv_bf, k_norm_g, 1.0, col0=0, n=d)
                v_sh = _mm_plain(src, w_kv_bf, BF16, col0=d, n=d)
            j = l - n_a
            q = _mm_headnorm(h, sb_w_q_bf, sb_q_norm_g[j], SB_HEAD_DIM ** -0.5, layer=j)
            o = _sb_attention(q, k_sh, v_sh)
            x = _mm_residual(o, sb_w_out_bf, x, gate, layer=j)
        x = _ffn_block(x, mod, l, ffn_w_in, ffn_conv_w, ffn_conv_b, ffn_w_out)
    return x.reshape(batch, s, d)
```

```python
import functools

import jax
import jax.numpy as jnp
from jax import lax
from jax.experimental import pallas as pl
from jax.experimental.pallas import tpu as pltpu

F32 = jnp.float32
BF16 = jnp.bfloat16

EPS = 1e-6
N_MOD = 6
CONV_WIDTH = 3
GLA_CHUNK = 64
GLA_HEADS = 4
GLA_TAU = 16.0
SB_HEAD_DIM = 128
SB_HEADS_PER_STEP = 2

V7X_LANES = 128
V7X_SUBLANES = 8
V7X_VMEM_LIMIT_BYTES = 56 * 1024 * 1024

F32_EXP_UNDERFLOW = -88.0


def _params(*semantics):
    return pltpu.CompilerParams(dimension_semantics=semantics,
                                vmem_limit_bytes=V7X_VMEM_LIMIT_BYTES)


def _tile(n, pref):
    if n <= pref:
        return n
    t = (pref // V7X_LANES) * V7X_LANES
    while t >= V7X_LANES:
        if n % t == 0:
            return t
        t -= V7X_LANES
    return n


def _dot(a, b):
    return jnp.dot(a, b, preferred_element_type=F32)


def _dot_nt(a, b):
    return lax.dot_general(a, b, (((1,), (1,)), ((), ())), preferred_element_type=F32)


def _dot_tn(a, b):
    return lax.dot_general(a, b, (((0,), (0,)), ((), ())), preferred_element_type=F32)


def _silu(x):
    return x / (1.0 + jnp.exp(-x))


def _split_bf16(x):
    hi = x.astype(BF16)
    lo = (x - hi.astype(F32)).astype(BF16)
    return hi, lo


def _cond_kernel(c_ref, w_ref, b_ref, tab_ref, o_ref):
    s = _silu(c_ref[...]).astype(BF16)
    acc = _dot(s, w_ref[...].astype(BF16))
    o_ref[...] = acc[0:1, :] + b_ref[...] + tab_ref[...]


def _cond(c, w_ada, b_ada, ada_table):
    d = c.shape[1]
    depth = ada_table.shape[0]
    n = w_ada.shape[1]
    tn = _tile(n, 512)
    c8 = jnp.broadcast_to(c, (V7X_SUBLANES, d))
    return pl.pallas_call(
        _cond_kernel,
        grid=(n // tn,),
        in_specs=[pl.BlockSpec((V7X_SUBLANES, d), lambda j: (0, 0)),
                  pl.BlockSpec((d, tn), lambda j: (0, j)),
                  pl.BlockSpec((1, tn), lambda j: (0, j)),
                  pl.BlockSpec((depth, tn), lambda j: (0, j))],
        out_specs=pl.BlockSpec((depth, tn), lambda j: (0, j)),
        out_shape=jax.ShapeDtypeStruct((depth, n), F32),
        compiler_params=_params("arbitrary"),
        name="cond",
    )(c8, w_ada, b_ada.reshape(1, n), ada_table.reshape(depth, n))


def _norm_affine_kernel(x_ref, mul_ref, add_ref, o_ref, *, plus_one):
    x = x_ref[...]
    ms = jnp.mean(x * x, axis=-1, keepdims=True)
    xn = x * lax.rsqrt(ms + EPS)
    mul = mul_ref[...]
    if plus_one:
        mul = 1.0 + mul
    o_ref[...] = (xn * mul + add_ref[...]).astype(o_ref.dtype)


def _norm_affine(x, mul, add, plus_one):
    s, d = x.shape
    tm = _tile(s, 256)
    return pl.pallas_call(
        functools.partial(_norm_affine_kernel, plus_one=plus_one),
        grid=(s // tm,),
        in_specs=[pl.BlockSpec((tm, d), lambda i: (i, 0)),
                  pl.BlockSpec((1, d), lambda i: (0, 0)),
                  pl.BlockSpec((1, d), lambda i: (0, 0))],
        out_specs=pl.BlockSpec((tm, d), lambda i: (i, 0)),
        out_shape=jax.ShapeDtypeStruct((s, d), BF16),
        compiler_params=_params("arbitrary"),
        name="norm_affine",
    )(x, mul.reshape(1, d), add.reshape(1, d))


def _mm_plain_kernel(a_ref, w_ref, o_ref, *, valid_cols=None):
    w = w_ref[...]
    if valid_cols is not None:
        col = lax.broadcasted_iota(jnp.int32, w.shape, 1)
        w = jnp.where(col < valid_cols, w, jnp.zeros_like(w))
    o_ref[...] = _dot(a_ref[...], w).astype(o_ref.dtype)


def _mm_headnorm_kernel(a_ref, w_ref, g_ref, o_ref, *, post_scale):
    acc = _dot(a_ref[...], w_ref[...])
    g = g_ref[...] * post_scale
    for c in range(acc.shape[1] // SB_HEAD_DIM):
        sl = slice(c * SB_HEAD_DIM, (c + 1) * SB_HEAD_DIM)
        blk = acc[:, sl]
        ms = jnp.mean(blk * blk, axis=-1, keepdims=True)
        o_ref[:, sl] = (blk * lax.rsqrt(ms + EPS) * g).astype(o_ref.dtype)


def _mm_residual_kernel(a_ref, w_ref, x_ref, gate_ref, o_ref):
    o_ref[...] = x_ref[...] + gate_ref[...] * _dot(a_ref[...], w_ref[...])


def _mm_tiles(m, n, k):
    tm = _tile(m, 1024)
    tn = _tile(n, 1024 if k <= 4096 else 512)
    if k > 4096:
        tm = _tile(m, 512)
    return tm, tn


def _weight_spec(w, layer, col0, tn):
    k = w.shape[-2]
    assert col0 % tn == 0
    c0 = col0 // tn
    if w.ndim == 2:
        return pl.BlockSpec((k, tn), lambda i, j: (0, c0 + j))
    return pl.BlockSpec((None, k, tn), lambda i, j: (layer, 0, c0 + j))


def _mm_plain(a, w, out_dtype, *, layer=0, col0=0, n=None, valid_cols=None):
    m, k = a.shape
    n = w.shape[-1] if n is None else n
    tm, tn = _mm_tiles(m, n, k)
    return pl.pallas_call(
        functools.partial(_mm_plain_kernel, valid_cols=valid_cols),
        grid=(m // tm, n // tn),
        in_specs=[pl.BlockSpec((tm, k), lambda i, j: (i, 0)),
                  _weight_spec(w, layer, col0, tn)],
        out_specs=pl.BlockSpec((tm, tn), lambda i, j: (i, j)),
        out_shape=jax.ShapeDtypeStruct((m, n), out_dtype),
        compiler_params=_params("arbitrary", "arbitrary"),
        name="mm_plain",
    )(a, w)


def _mm_headnorm(a, w, g, post_scale, *, layer=0, col0=0, n=None):
    m, k = a.shape
    n = w.shape[-1] if n is None else n
    tm, tn = _mm_tiles(m, n, k)
    return pl.pallas_call(
        functools.partial(_mm_headnorm_kernel, post_scale=post_scale),
        grid=(m // tm, n // tn),
        in_specs=[pl.BlockSpec((tm, k), lambda i, j: (i, 0)),
                  _weight_spec(w, layer, col0, tn),
                  pl.BlockSpec((1, SB_HEAD_DIM), lambda i, j: (0, 0))],
        out_specs=pl.BlockSpec((tm, tn), lambda i, j: (i, j)),
        out_shape=jax.ShapeDtypeStruct((m, n), BF16),
        compiler_params=_params("arbitrary", "arbitrary"),
        name="mm_headnorm",
    )(a, w, g.reshape(1, SB_HEAD_DIM))


def _mm_residual(a, w, x, gate, *, layer=0):
    m, k = a.shape
    n = w.shape[-1]
    tm, tn = _mm_tiles(m, n, k)
    return pl.pallas_call(
        _mm_residual_kernel,
        grid=(m // tm, n // tn),
        in_specs=[pl.BlockSpec((tm, k), lambda i, j: (i, 0)),
                  _weight_spec(w, layer, 0, tn),
                  pl.BlockSpec((tm, tn), lambda i, j: (i, j)),
                  pl.BlockSpec((1, tn), lambda i, j: (0, j))],
        out_specs=pl.BlockSpec((tm, tn), lambda i, j: (i, j)),
        out_shape=jax.ShapeDtypeStruct((m, n), F32),
        compiler_params=_params("arbitrary", "arbitrary"),
        name="mm_residual",
    )(a, w, x, gate.reshape(1, n))


FFN_CHUNK_ROWS = 128


def _ffn_in_kernel(h_ref, wa_ref, wb_ref, cwa_ref, cwb_ref, cba_ref, cbb_ref, *refs, n_side):
    side_f32_refs = refs[:n_side]
    o_ref = refs[n_side]
    side_bf_refs = refs[n_side + 1:2 * n_side + 1]
    wa_bf_ref, wb_bf_ref, carry_a_ref, carry_b_ref, ua_ref, ub_ref = refs[2 * n_side + 1:]
    tm = h_ref.shape[0]
    cm = min(FFN_CHUNK_ROWS, tm)

    for src_ref, dst_ref in zip(side_f32_refs, side_bf_refs):
        dst_ref[...] = src_ref[...].astype(BF16)

    @pl.when(pl.program_id(1) == 0)
    def _():
        wa_bf_ref[...] = wa_ref[...].astype(BF16)
        wb_bf_ref[...] = wb_ref[...].astype(BF16)
        carry_a_ref[...] = jnp.zeros_like(carry_a_ref)
        carry_b_ref[...] = jnp.zeros_like(carry_b_ref)

    row = lax.broadcasted_iota(jnp.int32, carry_a_ref.shape, 0)

    def conv(u, prev, cw_ref, cb_ref):
        w0, w1, w2 = cw_ref[0:1, :], cw_ref[1:2, :], cw_ref[2:3, :]
        bias = cb_ref[...]
        u1 = pltpu.roll(u, 1, axis=0)
        u2 = pltpu.roll(u, 2, axis=0)
        head1 = jnp.where(row < 1, pltpu.roll(prev, 1, axis=0), u1[0:V7X_SUBLANES])
        head2 = jnp.where(row < 2, pltpu.roll(prev, 2, axis=0), u2[0:V7X_SUBLANES])
        head = bias + head2 * w0 + head1 * w1 + u[0:V7X_SUBLANES] * w2
        tail = (bias + u2[V7X_SUBLANES:] * w0 + u1[V7X_SUBLANES:] * w1
                + u[V7X_SUBLANES:] * w2)
        return head, tail

    prev_a = carry_a_ref[...]
    prev_b = carry_b_ref[...]
    for c in range(tm // cm):
        r0 = c * cm
        h = h_ref[r0:r0 + cm, :]
        slot = lax.rem(pl.program_id(1) + c, 2)
        ua_ref[slot] = _dot(h, wa_bf_ref[...])
        ub_ref[slot] = _dot(h, wb_bf_ref[...])
        ua = ua_ref[slot]
        ub = ub_ref[slot]
        a_head, a_tail = conv(ua, prev_a, cwa_ref, cba_ref)
        b_head, b_tail = conv(ub, prev_b, cwb_ref, cbb_ref)
        o_ref[r0:r0 + V7X_SUBLANES, :] = (_silu(a_head) * b_head).astype(o_ref.dtype)
        o_ref[r0 + V7X_SUBLANES:r0 + cm, :] = (_silu(a_tail) * b_tail).astype(o_ref.dtype)
        prev_a = ua[cm - V7X_SUBLANES:cm]
        prev_b = ub[cm - V7X_SUBLANES:cm]
    carry_a_ref[...] = prev_a
    carry_b_ref[...] = prev_b


def _slab_rows(rows, steps):
    packed = 2 * V7X_SUBLANES
    return min(m for m in range(packed, rows + 1, packed) if rows % m == 0 and m * steps >= rows)


def _ffn_in(h, w_in, conv_w, conv_b, layer, to_round):
    s, d = h.shape
    f = w_in.shape[-1] // 2
    tm = _tile(s, 1024)
    tn = _tile(f, 256)
    nj = f // tn
    ni = s // tm
    cm = min(FFN_CHUNK_ROWS, tm)
    conv_b = conv_b.reshape(conv_b.shape[0], 1, 2 * f)

    side_in_specs, side_out_specs, side_out_shapes = [], [], []
    for w, w_layer in to_round:
        rows, cols = w.shape[-2:]
        slab = _slab_rows(rows, nj * ni)
        last = rows // slab - 1

        def slab_index(j, i, last=last):
            return jnp.minimum(j * ni + i, last)

        side_in_specs.append(pl.BlockSpec(
            (None, slab, cols), lambda j, i, w_layer=w_layer, f=slab_index: (w_layer, f(j, i), 0)))
        side_out_specs.append(pl.BlockSpec((slab, cols), lambda j, i, f=slab_index: (f(j, i), 0)))
        side_out_shapes.append(jax.ShapeDtypeStruct((rows, cols), BF16))

    outs = pl.pallas_call(
        functools.partial(_ffn_in_kernel, n_side=len(to_round)),
        grid=(nj, ni),
        in_specs=[pl.BlockSpec((tm, d), lambda j, i: (i, 0)),
                  pl.BlockSpec((None, d, tn), lambda j, i: (layer, 0, j)),
                  pl.BlockSpec((None, d, tn), lambda j, i: (layer, 0, j + nj)),
                  pl.BlockSpec((None, CONV_WIDTH, tn), lambda j, i: (layer, 0, j)),
                  pl.BlockSpec((None, CONV_WIDTH, tn), lambda j, i: (layer, 0, j + nj)),
                  pl.BlockSpec((None, 1, tn), lambda j, i: (layer, 0, j)),
                  pl.BlockSpec((None, 1, tn), lambda j, i: (layer, 0, j + nj))] + side_in_specs,
        out_specs=[pl.BlockSpec((tm, tn), lambda j, i: (i, j))] + side_out_specs,
        out_shape=[jax.ShapeDtypeStruct((s, f), BF16)] + side_out_shapes,
        scratch_shapes=[pltpu.VMEM((d, tn), BF16),
                        pltpu.VMEM((d, tn), BF16),
                        pltpu.VMEM((V7X_SUBLANES, tn), F32),
                        pltpu.VMEM((V7X_SUBLANES, tn), F32),
                        pltpu.VMEM((2, cm, tn), F32),
                        pltpu.VMEM((2, cm, tn), F32)],
        compiler_params=_params("arbitrary", "arbitrary"),
        name="ffn_in",
    )(h, w_in, w_in, conv_w, conv_w, conv_b, conv_b, *(w for w, _ in to_round))
    return outs[0], list(outs[1:])


def _gla_kernel(qk_ref, v_ref, r_ref, gl_ref, wgu_ref, bg_ref, ng_ref, tril_ref,
                o_ref, state_ref, *, pairs, dk, dv):
    kw = GLA_HEADS * dk
    pair_rows = 2 * GLA_CHUNK

    @pl.when(pl.program_id(0) == 0)
    def _():
        state_ref[...] = jnp.zeros_like(state_ref)

    tril = tril_ref[...]
    own_causal = tril > 0
    r_id = lax.broadcasted_iota(jnp.int32, (pair_rows, pair_rows), 0)
    c_id = lax.broadcasted_iota(jnp.int32, (pair_rows, pair_rows), 1)
    b_sees_a = jnp.logical_and(r_id >= GLA_CHUNK, c_id < GLA_CHUNK)
    in_b = lax.broadcasted_iota(jnp.int32, (pair_rows, 1), 0) >= GLA_CHUNK
    q_scale = dk ** -0.5

    def pair(p, carry):
        rows = pl.ds(pl.multiple_of(p * pair_rows, pair_rows), pair_rows)
        y = _dot(gl_ref[rows, :].astype(BF16), wgu_ref[...]) + bg_ref[...]
        log_alpha = (jnp.minimum(y, 0.0) - jnp.log(1.0 + jnp.exp(-jnp.abs(y)))) * (1.0 / GLA_TAU)
        hi, lo = _split_bf16(log_alpha)
        b_cum = _dot(tril, hi) + _dot(tril, lo)
        b_last_a = b_cum[GLA_CHUNK - 1:GLA_CHUNK, :]
        b_last_b = b_cum[pair_rows - 1:pair_rows, :]
        d_a = jnp.exp(b_last_a)
        d_b = jnp.exp(b_last_b)
        e_pos = jnp.exp(b_cum)
        e_neg = jnp.exp(-b_cum)
        e_end = jnp.exp(jnp.where(in_b, b_last_b, b_last_a) - b_cum)
        e_pos_s = e_pos * jnp.where(in_b, d_a, 1.0)
        e_end_s = e_end * jnp.where(in_b, 1.0, d_b)
        decay_rows = jnp.broadcast_to(d_a * d_b, (V7X_LANES, kw))
        for h in range(GLA_HEADS):
            ksl = slice(h * dk, (h + 1) * dk)
            vsl = slice(h * dv, (h + 1) * dv)
            q = qk_ref[rows, ksl] * q_scale
            k = qk_ref[rows, kw + h * dk:kw + (h + 1) * dk]
            q_dec = (q * e_pos[:, ksl]).astype(BF16)
            k_dec = (k * e_neg[:, ksl]).astype(BF16)
            k_end = (k * e_end[:, ksl]).astype(BF16)
            q_dec_s = (q * e_pos_s[:, ksl]).astype(BF16)
            k_end_s = (k * e_end_s[:, ksl]).astype(BF16)
            v = v_ref[rows, vsl]
            scores = jnp.where(own_causal, _dot_nt(q_dec, k_dec),
                               jnp.where(b_sees_a, _dot_nt(q_dec, k_end), 0.0)).astype(BF16)
            state = state_ref[h]
            o = _dot(scores, v) + _dot(q_dec_s, state.astype(BF16))
            decay = jnp.concatenate(
                [decay_rows[:, h * dk + b * V7X_LANES:h * dk + (b + 1) * V7X_LANES].T
                 for b in range(dk // V7X_LANES)], axis=0)
            decay = jnp.concatenate([decay] * (dv // V7X_LANES), axis=1)
            state_ref[h] = state * decay + _dot_tn(k_end_s, v)
            ms = jnp.mean(o * o, axis=-1, keepdims=True)
            o = o * lax.rsqrt(ms + EPS) * ng_ref[...]
            o_ref[rows, vsl] = (_silu(r_ref[rows, vsl]) * o).astype(o_ref.dtype)
        return carry

    lax.fori_loop(0, pairs, pair, 0)


def _gla(qk, v, r, g_low, w_gate_up, b_gate, norm_g):
    s = qk.shape[0]
    kw = w_gate_up.shape[1]
    vw = v.shape[1]
    dk, dv = kw // GLA_HEADS, vw // GLA_HEADS
    rows = _tile(s, 256)
    pair_rows = 2 * GLA_CHUNK
    pairs = rows // pair_rows
    t = jnp.arange(pair_rows)
    tril = jnp.logical_and(t[:, None] >= t[None, :],
                           t[:, None] // GLA_CHUNK == t[None, :] // GLA_CHUNK).astype(BF16)
    return pl.pallas_call(
        functools.partial(_gla_kernel, pairs=pairs, dk=dk, dv=dv),
        grid=(s // rows,),
        in_specs=[pl.BlockSpec((rows, 2 * kw), lambda i: (i, 0)),
                  pl.BlockSpec((rows, vw), lambda i: (i, 0)),
                  pl.BlockSpec((rows, vw), lambda i: (i, 0)),
                  pl.BlockSpec((rows, V7X_LANES), lambda i: (i, 0)),
                  pl.BlockSpec((V7X_LANES, kw), lambda i: (0, 0)),
                  pl.BlockSpec((1, kw), lambda i: (0, 0)),
                  pl.BlockSpec((1, dv), lambda i: (0, 0)),
                  pl.BlockSpec((pair_rows, pair_rows), lambda i: (0, 0))],
        out_specs=pl.BlockSpec((rows, vw), lambda i: (i, 0)),
        out_shape=jax.ShapeDtypeStruct((s, vw), BF16),
        scratch_shapes=[pltpu.VMEM((GLA_HEADS, dk, dv), F32)],
        compiler_params=_params("arbitrary"),
        name="gla",
    )(qk, v, r, g_low, w_gate_up, b_gate.reshape(1, kw), norm_g.reshape(1, dv), tril)


def _sb_kernel(q_ref, k_ref, v_ref, t_ref, o_ref, acc_ref, run_ref, *, tq, tk, heads):
    qi = pl.program_id(1)
    tmat = t_ref[...]
    row_groups = tq // tk
    base = qi * row_groups
    below_diag = (lax.broadcasted_iota(jnp.int32, (tk, tk), 1)
                  < lax.broadcasted_iota(jnp.int32, (tk, tk), 0))

    chains = heads * row_groups
    all_chains = range(chains)
    group = [c % row_groups for c in all_chains]
    rows = [slice(g * tk, (g + 1) * tk) for g in group]
    lanes = [slice((c // row_groups) * SB_HEAD_DIM, (c // row_groups + 1) * SB_HEAD_DIM)
             for c in all_chains]

    def sweep(blocks, diagonal, live):
        starts = [pl.multiple_of(blocks[g] * tk, tk) for g in group]
        if live is not None:
            live = [live[g] for g in group]
        zs = [_dot_nt(q_ref[rows[c], lanes[c]], k_ref[pl.ds(starts[c], tk), lanes[c]])
              for c in all_chains]
        log_betas, halves = [], []
        for z in zs:
            softplus = jnp.maximum(z, 0.0) + jnp.log(1.0 + jnp.exp(-jnp.abs(z)))
            log_betas.append(z - softplus)
            neg_log_rest = softplus
            if diagonal:
                neg_log_rest = jnp.where(below_diag, neg_log_rest, 0.0)
            halves.append(jnp.concatenate(_split_bf16(neg_log_rest), axis=1))
        cms = [_dot(hl, tmat) for hl in halves]
        atts, runs = [], []
        for c in all_chains:
            log_att = log_betas[c] + cms[c][:, :tk]
            if diagonal:
                atts.append(jnp.where(below_diag, jnp.exp(log_att), 0.0))
                runs.append(cms[c][:, tk:])
            else:
                run = jnp.where(live[c], run_ref[rows[c], lanes[c]], -1e30)
                atts.append(jnp.exp(log_att + run))
                runs.append(run + cms[c][:, tk:])
        pvs = [_dot(atts[c].astype(BF16), v_ref[pl.ds(starts[c], tk), lanes[c]])
               for c in all_chains]
        if not diagonal:
            pvs = [acc_ref[rows[c], lanes[c]] + pvs[c] for c in all_chains]
        for c in all_chains:
            acc_ref[rows[c], lanes[c]] = pvs[c]
            run_ref[rows[c], lanes[c]] = runs[c]
        return jnp.max(functools.reduce(jnp.maximum, runs))

    top = sweep([base + g for g in range(row_groups)], True, None)

    def cond(st):
        n, top = st
        return jnp.logical_and(n <= base + row_groups - 1, top > F32_EXP_UNDERFLOW)

    def body(st):
        n, _ = st
        blocks = [base + g - n for g in range(row_groups)]
        top = sweep([jnp.maximum(j, 0) for j in blocks], False, [j >= 0 for j in blocks])
        return n + 1, top

    lax.while_loop(cond, body, (jnp.int32(1), top))
    o_ref[...] = acc_ref[...].astype(o_ref.dtype)


def _sb_attention(q, k, v):
    s, d = q.shape
    dh = SB_HEAD_DIM
    tk = V7X_LANES
    tq = _tile(s, 1024)
    r = lax.broadcasted_iota(jnp.int32, (tk, 2 * tk), 0)
    c = lax.broadcasted_iota(jnp.int32, (tk, 2 * tk), 1)
    half = jnp.where(jnp.logical_or(c >= tk, r > c), -1.0, 0.0).astype(BF16)
    tmat = jnp.concatenate([half, half], axis=0)
    heads = SB_HEADS_PER_STEP
    wh = heads * dh
    return pl.pallas_call(
        functools.partial(_sb_kernel, tq=tq, tk=tk, heads=heads),
        grid=(d // wh, s // tq),
        in_specs=[pl.BlockSpec((tq, wh), lambda h, i: (i, h)),
                  pl.BlockSpec((s, wh), lambda h, i: (0, h)),
                  pl.BlockSpec((s, wh), lambda h, i: (0, h)),
                  pl.BlockSpec((2 * tk, 2 * tk), lambda h, i: (0, 0))],
        out_specs=pl.BlockSpec((tq, wh), lambda h, i: (i, h)),
        out_shape=jax.ShapeDtypeStruct((s, d), BF16),
        scratch_shapes=[pltpu.VMEM((tq, wh), F32), pltpu.VMEM((tq, wh), F32)],
        compiler_params=_params("arbitrary", "arbitrary"),
        name="sb_attention",
    )(q, k, v, tmat)


def kernel(x, c, w_ada, b_ada, ada_table, gla_w_in, gla_w_gate_up, gla_b_gate, gla_norm_g, gla_w_out, kv_norm_g, w_kv, k_norm_g, sb_w_q, sb_q_norm_g, sb_w_out, ffn_w_in, ffn_conv_w, ffn_conv_b, ffn_w_out):
    batch, s, d = x.shape
    assert batch == 1, "adaLN modulation rows are built for a single sequence"
    depth = ada_table.shape[0]
    n_a = gla_w_in.shape[0]
    kw = gla_w_gate_up.shape[2]
    rank = gla_w_gate_up.shape[1]
    x = x.reshape(s, d)

    w_up_bf = jnp.pad(gla_w_gate_up, ((0, 0), (0, V7X_LANES - rank), (0, 0))).astype(BF16)
    w_kv3 = w_kv.reshape(1, d, 2 * d)

    def mixer_weights(l):
        if l >= depth:
            return []
        if l < n_a:
            return [(gla_w_in, l), (gla_w_out, l)]
        own = [(sb_w_q, l - n_a), (sb_w_out, l - n_a)]
        return ([(w_kv3, 0)] if l == n_a else []) + own

    mixer_bf = [w[i].astype(BF16) for w, i in mixer_weights(0)]

    mods = _cond(c, w_ada, b_ada, ada_table)
    k_sh = v_sh = None
    for l in range(depth):
        mod = mods[l]
        shift, scale, gate = (mod[i * d:(i + 1) * d] for i in (0, 1, 2))
        h = _norm_affine(x, scale, shift, True)
        if l < n_a:
            w_in_bf, w_out_bf = mixer_bf
            qk = _mm_plain(h, w_in_bf, F32, col0=0, n=2 * kw)
            v = _mm_plain(h, w_in_bf, BF16, col0=2 * kw, n=d)
            r = _mm_plain(h, w_in_bf, F32, col0=2 * kw + d, n=d)
            g_low = _mm_plain(h, w_in_bf, F32, col0=2 * kw + 2 * d, n=V7X_LANES, valid_cols=rank)
            o = _gla(qk, v, r, g_low, w_up_bf[l], gla_b_gate[l], gla_norm_g[l])
        else:
            if l == n_a:
                w_kv_bf = mixer_bf.pop(0)
                src = _norm_affine(x, kv_norm_g, jnp.zeros_like(kv_norm_g), False)
                k_sh = _mm_headnorm(src, w_kv_bf, k_norm_g, 1.0, col0=0, n=d)
                v_sh = _mm_plain(src, w_kv_bf, BF16, col0=d, n=d)
            w_q_bf, w_out_bf = mixer_bf
            q = _mm_headnorm(h, w_q_bf, sb_q_norm_g[l - n_a], SB_HEAD_DIM ** -0.5)
            o = _sb_attention(q, k_sh, v_sh)
        x = _mm_residual(o, w_out_bf, x, gate)

        shift, scale, gate = (mod[i * d:(i + 1) * d] for i in (3, 4, 5))
        h = _norm_affine(x, scale, shift, True)
        g, rounded = _ffn_in(h, ffn_w_in, ffn_conv_w, ffn_conv_b, l,
                             [(ffn_w_out, l)] + mixer_weights(l + 1))
        x = _mm_residual(g, rounded[0], x, gate)
        mixer_bf = rounded[1:]
    return x.reshape(batch, s, d)
```

```python
import functools

import jax
import jax.numpy as jnp
from jax import lax
from jax.experimental import pallas as pl
from jax.experimental.pallas import tpu as pltpu

F32 = jnp.float32
BF16 = jnp.bfloat16

EPS = 1e-6
N_MOD = 6
CONV_WIDTH = 3
GLA_CHUNK = 64
GLA_HEADS = 4
GLA_TAU = 16.0
SB_HEAD_DIM = 128
SB_HEADS_PER_STEP = 2

V7X_LANES = 128
V7X_SUBLANES = 8
V7X_VMEM_LIMIT_BYTES = 56 * 1024 * 1024

F32_EXP_UNDERFLOW = -88.0


def _params(*semantics):
    return pltpu.CompilerParams(dimension_semantics=semantics,
                                vmem_limit_bytes=V7X_VMEM_LIMIT_BYTES)


def _tile(n, pref):
    if n <= pref:
        return n
    t = (pref // V7X_LANES) * V7X_LANES
    while t >= V7X_LANES:
        if n % t == 0:
            return t
        t -= V7X_LANES
    return n


def _dot(a, b):
    return jnp.dot(a, b, preferred_element_type=F32)


def _dot_nt(a, b):
    return lax.dot_general(a, b, (((1,), (1,)), ((), ())), preferred_element_type=F32)


def _dot_tn(a, b):
    return lax.dot_general(a, b, (((0,), (0,)), ((), ())), preferred_element_type=F32)


def _silu(x):
    return x / (1.0 + jnp.exp(-x))


def _split_bf16(x):
    hi = x.astype(BF16)
    lo = (x - hi.astype(F32)).astype(BF16)
    return hi, lo


def _cond_kernel(c_ref, w_ref, b_ref, tab_ref, o_ref):
    s = _silu(c_ref[...]).astype(BF16)
    acc = _dot(s, w_ref[...].astype(BF16))
    o_ref[...] = acc[0:1, :] + b_ref[...] + tab_ref[...]


def _cond(c, w_ada, b_ada, ada_table):
    d = c.shape[1]
    depth = ada_table.shape[0]
    n = w_ada.shape[1]
    tn = _tile(n, 512)
    c8 = jnp.broadcast_to(c, (V7X_SUBLANES, d))
    return pl.pallas_call(
        _cond_kernel,
        grid=(n // tn,),
        in_specs=[pl.BlockSpec((V7X_SUBLANES, d), lambda j: (0, 0)),
                  pl.BlockSpec((d, tn), lambda j: (0, j)),
                  pl.BlockSpec((1, tn), lambda j: (0, j)),
                  pl.BlockSpec((depth, tn), lambda j: (0, j))],
        out_specs=pl.BlockSpec((depth, tn), lambda j: (0, j)),
        out_shape=jax.ShapeDtypeStruct((depth, n), F32),
        compiler_params=_params("arbitrary"),
        name="cond",
    )(c8, w_ada, b_ada.reshape(1, n), ada_table.reshape(depth, n))


def _norm_affine_kernel(x_ref, mul_ref, add_ref, o_ref, *, plus_one):
    x = x_ref[...]
    ms = jnp.mean(x * x, axis=-1, keepdims=True)
    xn = x * lax.rsqrt(ms + EPS)
    mul = mul_ref[...]
    if plus_one:
        mul = 1.0 + mul
    o_ref[...] = (xn * mul + add_ref[...]).astype(o_ref.dtype)


def _norm_affine(x, mul, add, plus_one):
    s, d = x.shape
    tm = _tile(s, 256)
    return pl.pallas_call(
        functools.partial(_norm_affine_kernel, plus_one=plus_one),
        grid=(s // tm,),
        in_specs=[pl.BlockSpec((tm, d), lambda i: (i, 0)),
                  pl.BlockSpec((1, d), lambda i: (0, 0)),
                  pl.BlockSpec((1, d), lambda i: (0, 0))],
        out_specs=pl.BlockSpec((tm, d), lambda i: (i, 0)),
        out_shape=jax.ShapeDtypeStruct((s, d), BF16),
        compiler_params=_params("arbitrary"),
        name="norm_affine",
    )(x, mul.reshape(1, d), add.reshape(1, d))


def _mm_plain_kernel(a_ref, w_ref, o_ref, *, valid_cols=None):
    w = w_ref[...]
    if valid_cols is not None:
        col = lax.broadcasted_iota(jnp.int32, w.shape, 1)
        w = jnp.where(col < valid_cols, w, jnp.zeros_like(w))
    o_ref[...] = _dot(a_ref[...], w).astype(o_ref.dtype)


def _mm_headnorm_kernel(a_ref, w_ref, g_ref, o_ref, *, post_scale):
    acc = _dot(a_ref[...], w_ref[...])
    g = g_ref[...] * post_scale
    for c in range(acc.shape[1] // SB_HEAD_DIM):
        sl = slice(c * SB_HEAD_DIM, (c + 1) * SB_HEAD_DIM)
        blk = acc[:, sl]
        ms = jnp.mean(blk * blk, axis=-1, keepdims=True)
        o_ref[:, sl] = (blk * lax.rsqrt(ms + EPS) * g).astype(o_ref.dtype)


def _mm_residual_kernel(a_ref, w_ref, x_ref, gate_ref, o_ref):
    o_ref[...] = x_ref[...] + gate_ref[...] * _dot(a_ref[...], w_ref[...])


def _mm_tiles(m, n, k):
    tm = _tile(m, 1024)
    tn = _tile(n, 1024 if k <= 4096 else 512)
    if k > 4096:
        tm = _tile(m, 512)
    return tm, tn


def _weight_spec(w, layer, col0, tn):
    k = w.shape[-2]
    assert col0 % tn == 0
    c0 = col0 // tn
    if w.ndim == 2:
        return pl.BlockSpec((k, tn), lambda i, j: (0, c0 + j))
    return pl.BlockSpec((None, k, tn), lambda i, j: (layer, 0, c0 + j))


def _mm_plain(a, w, out_dtype, *, layer=0, col0=0, n=None, valid_cols=None):
    m, k = a.shape
    n = w.shape[-1] if n is None else n
    tm, tn = _mm_tiles(m, n, k)
    return pl.pallas_call(
        functools.partial(_mm_plain_kernel, valid_cols=valid_cols),
        grid=(m // tm, n // tn),
        in_specs=[pl.BlockSpec((tm, k), lambda i, j: (i, 0)),
                  _weight_spec(w, layer, col0, tn)],
        out_specs=pl.BlockSpec((tm, tn), lambda i, j: (i, j)),
        out_shape=jax.ShapeDtypeStruct((m, n), out_dtype),
        compiler_params=_params("arbitrary", "arbitrary"),
        name="mm_plain",
    )(a, w)


def _mm_headnorm(a, w, g, post_scale, *, layer=0, col0=0, n=None):
    m, k = a.shape
    n = w.shape[-1] if n is None else n
    tm, tn = _mm_tiles(m, n, k)
    return pl.pallas_call(
        functools.partial(_mm_headnorm_kernel, post_scale=post_scale),
        grid=(m // tm, n // tn),
        in_specs=[pl.BlockSpec((tm, k), lambda i, j: (i, 0)),
                  _weight_spec(w, layer, col0, tn),
                  pl.BlockSpec((1, SB_HEAD_DIM), lambda i, j: (0, 0))],
        out_specs=pl.BlockSpec((tm, tn), lambda i, j: (i, j)),
        out_shape=jax.ShapeDtypeStruct((m, n), BF16),
        compiler_params=_params("arbitrary", "arbitrary"),
        name="mm_headnorm",
    )(a, w, g.reshape(1, SB_HEAD_DIM))


def _mm_residual(a, w, x, gate, *, layer=0):
    m, k = a.shape
    n = w.shape[-1]
    tm, tn = _mm_tiles(m, n, k)
    return pl.pallas_call(
        _mm_residual_kernel,
        grid=(m // tm, n // tn),
        in_specs=[pl.BlockSpec((tm, k), lambda i, j: (i, 0)),
                  _weight_spec(w, layer, 0, tn),
                  pl.BlockSpec((tm, tn), lambda i, j: (i, j)),
                  pl.BlockSpec((1, tn), lambda i, j: (0, j))],
        out_specs=pl.BlockSpec((tm, tn), lambda i, j: (i, j)),
        out_shape=jax.ShapeDtypeStruct((m, n), F32),
        compiler_params=_params("arbitrary", "arbitrary"),
        name="mm_residual",
    )(a, w, x, gate.reshape(1, n))


FFN_CHUNK_ROWS = 128


def _ffn_in_kernel(h_ref, wa_ref, wb_ref, cwa_ref, cwb_ref, cba_ref, cbb_ref, *refs, n_side):
    side_f32_refs = refs[:n_side]
    o_ref = refs[n_side]
    side_bf_refs = refs[n_side + 1:2 * n_side + 1]
    wa_bf_ref, wb_bf_ref, carry_a_ref, carry_b_ref, ua_ref, ub_ref = refs[2 * n_side + 1:]
    tm = h_ref.shape[0]
    cm = min(FFN_CHUNK_ROWS, tm)

    for src_ref, dst_ref in zip(side_f32_refs, side_bf_refs):
        dst_ref[...] = src_ref[...].astype(BF16)

    @pl.when(pl.program_id(1) == 0)
    def _():
        wa_bf_ref[...] = wa_ref[...].astype(BF16)
        wb_bf_ref[...] = wb_ref[...].astype(BF16)
        carry_a_ref[...] = jnp.zeros_like(carry_a_ref)
        carry_b_ref[...] = jnp.zeros_like(carry_b_ref)

    row = lax.broadcasted_iota(jnp.int32, carry_a_ref.shape, 0)

    def conv(u, prev, cw_ref, cb_ref):
        w0, w1, w2 = cw_ref[0:1, :], cw_ref[1:2, :], cw_ref[2:3, :]
        bias = cb_ref[...]
        u1 = pltpu.roll(u, 1, axis=0)
        u2 = pltpu.roll(u, 2, axis=0)
        head1 = jnp.where(row < 1, pltpu.roll(prev, 1, axis=0), u1[0:V7X_SUBLANES])
        head2 = jnp.where(row < 2, pltpu.roll(prev, 2, axis=0), u2[0:V7X_SUBLANES])
        head = bias + head2 * w0 + head1 * w1 + u[0:V7X_SUBLANES] * w2
        tail = (bias + u2[V7X_SUBLANES:] * w0 + u1[V7X_SUBLANES:] * w1
                + u[V7X_SUBLANES:] * w2)
        return head, tail

    prev_a = carry_a_ref[...]
    prev_b = carry_b_ref[...]
    for c in range(tm // cm):
        r0 = c * cm
        h = h_ref[r0:r0 + cm, :]
        slot = lax.rem(pl.program_id(1) + c, 2)
        ua_ref[slot] = _dot(h, wa_bf_ref[...])
        ub_ref[slot] = _dot(h, wb_bf_ref[...])
        ua = ua_ref[slot]
        ub = ub_ref[slot]
        a_head, a_tail = conv(ua, prev_a, cwa_ref, cba_ref)
        b_head, b_tail = conv(ub, prev_b, cwb_ref, cbb_ref)
        o_ref[r0:r0 + V7X_SUBLANES, :] = (_silu(a_head) * b_head).astype(o_ref.dtype)
        o_ref[r0 + V7X_SUBLANES:r0 + cm, :] = (_silu(a_tail) * b_tail).astype(o_ref.dtype)
        prev_a = ua[cm - V7X_SUBLANES:cm]
        prev_b = ub[cm - V7X_SUBLANES:cm]
    carry_a_ref[...] = prev_a
    carry_b_ref[...] = prev_b


def _slab_rows(rows, steps):
    packed = 2 * V7X_SUBLANES
    return min(m for m in range(packed, rows + 1, packed) if rows % m == 0 and m * steps >= rows)


def _ffn_in(h, w_in, conv_w, conv_b, layer, to_round):
    s, d = h.shape
    f = w_in.shape[-1] // 2
    tm = _tile(s, 1024)
    tn = _tile(f, 256)
    nj = f // tn
    ni = s // tm
    cm = min(FFN_CHUNK_ROWS, tm)
    conv_b = conv_b.reshape(conv_b.shape[0], 1, 2 * f)

    side_in_specs, side_out_specs, side_out_shapes = [], [], []
    for w, w_layer in to_round:
        rows, cols = w.shape[-2:]
        slab = _slab_rows(rows, nj * ni)
        last = rows // slab - 1

        def slab_index(j, i, last=last):
            return jnp.minimum(j * ni + i, last)

        side_in_specs.append(pl.BlockSpec(
            (None, slab, cols), lambda j, i, w_layer=w_layer, f=slab_index: (w_layer, f(j, i), 0)))
        side_out_specs.append(pl.BlockSpec((slab, cols), lambda j, i, f=slab_index: (f(j, i), 0)))
        side_out_shapes.append(jax.ShapeDtypeStruct((rows, cols), BF16))

    outs = pl.pallas_call(
        functools.partial(_ffn_in_kernel, n_side=len(to_round)),
        grid=(nj, ni),
        in_specs=[pl.BlockSpec((tm, d), lambda j, i: (i, 0)),
                  pl.BlockSpec((None, d, tn), lambda j, i: (layer, 0, j)),
                  pl.BlockSpec((None, d, tn), lambda j, i: (layer, 0, j + nj)),
                  pl.BlockSpec((None, CONV_WIDTH, tn), lambda j, i: (layer, 0, j)),
                  pl.BlockSpec((None, CONV_WIDTH, tn), lambda j, i: (layer, 0, j + nj)),
                  pl.BlockSpec((None, 1, tn), lambda j, i: (layer, 0, j)),
                  pl.BlockSpec((None, 1, tn), lambda j, i: (layer, 0, j + nj))] + side_in_specs,
        out_specs=[pl.BlockSpec((tm, tn), lambda j, i: (i, j))] + side_out_specs,
        out_shape=[jax.ShapeDtypeStruct((s, f), BF16)] + side_out_shapes,
        scratch_shapes=[pltpu.VMEM((d, tn), BF16),
                        pltpu.VMEM((d, tn), BF16),
                        pltpu.VMEM((V7X_SUBLANES, tn), F32),
                        pltpu.VMEM((V7X_SUBLANES, tn), F32),
                        pltpu.VMEM((2, cm, tn), F32),
                        pltpu.VMEM((2, cm, tn), F32)],
        compiler_params=_params("arbitrary", "arbitrary"),
        name="ffn_in",
    )(h, w_in, w_in, conv_w, conv_w, conv_b, conv_b, *(w for w, _ in to_round))
    return outs[0], list(outs[1:])


def _gla_kernel(qk_ref, v_ref, r_ref, gl_ref, wgu_ref, bg_ref, ng_ref, tril_ref,
                o_ref, state_ref, *, pairs, dk, dv):
    kw = GLA_HEADS * dk
    pair_rows = 2 * GLA_CHUNK

    @pl.when(pl.program_id(0) == 0)
    def _():
        state_ref[...] = jnp.zeros_like(state_ref)

    tril = tril_ref[...]
    own_causal = tril > 0
    r_id = lax.broadcasted_iota(jnp.int32, (pair_rows, pair_rows), 0)
    c_id = lax.broadcasted_iota(jnp.int32, (pair_rows, pair_rows), 1)
    b_sees_a = jnp.logical_and(r_id >= GLA_CHUNK, c_id < GLA_CHUNK)
    in_b = lax.broadcasted_iota(jnp.int32, (pair_rows, 1), 0) >= GLA_CHUNK
    q_scale = dk ** -0.5

    def pair(p, carry):
        rows = pl.ds(pl.multiple_of(p * pair_rows, pair_rows), pair_rows)
        y = _dot(gl_ref[rows, :].astype(BF16), wgu_ref[...]) + bg_ref[...]
        log_alpha = (jnp.minimum(y, 0.0) - jnp.log(1.0 + jnp.exp(-jnp.abs(y)))) * (1.0 / GLA_TAU)
        hi, lo = _split_bf16(log_alpha)
        b_cum = _dot(tril, hi) + _dot(tril, lo)
        b_last_a = b_cum[GLA_CHUNK - 1:GLA_CHUNK, :]
        b_last_b = b_cum[pair_rows - 1:pair_rows, :]
        d_a = jnp.exp(b_last_a)
        d_b = jnp.exp(b_last_b)
        e_pos = jnp.exp(b_cum)
        e_neg = jnp.exp(-b_cum)
        e_end = jnp.exp(jnp.where(in_b, b_last_b, b_last_a) - b_cum)
        e_pos_s = e_pos * jnp.where(in_b, d_a, 1.0)
        e_end_s = e_end * jnp.where(in_b, 1.0, d_b)
        decay_rows = jnp.broadcast_to(d_a * d_b, (V7X_LANES, kw))
        for h in range(GLA_HEADS):
            ksl = slice(h * dk, (h + 1) * dk)
            vsl = slice(h * dv, (h + 1) * dv)
            q = qk_ref[rows, ksl] * q_scale
            k = qk_ref[rows, kw + h * dk:kw + (h + 1) * dk]
            q_dec = (q * e_pos[:, ksl]).astype(BF16)
            k_dec = (k * e_neg[:, ksl]).astype(BF16)
            k_end = (k * e_end[:, ksl]).astype(BF16)
            q_dec_s = (q * e_pos_s[:, ksl]).astype(BF16)
            k_end_s = (k * e_end_s[:, ksl]).astype(BF16)
            v = v_ref[rows, vsl]
            scores = jnp.where(own_causal, _dot_nt(q_dec, k_dec),
                               jnp.where(b_sees_a, _dot_nt(q_dec, k_end), 0.0)).astype(BF16)
            state = state_ref[h]
            o = _dot(scores, v) + _dot(q_dec_s, state.astype(BF16))
            decay = jnp.concatenate(
                [decay_rows[:, h * dk + b * V7X_LANES:h * dk + (b + 1) * V7X_LANES].T
                 for b in range(dk // V7X_LANES)], axis=0)
            decay = jnp.concatenate([decay] * (dv // V7X_LANES), axis=1)
            state_ref[h] = state * decay + _dot_tn(k_end_s, v)
            ms = jnp.mean(o * o, axis=-1, keepdims=True)
            o = o * lax.rsqrt(ms + EPS) * ng_ref[...]
            o_ref[rows, vsl] = (_silu(r_ref[rows, vsl]) * o).astype(o_ref.dtype)
        return carry

    lax.fori_loop(0, pairs, pair, 0)


def _gla(qk, v, r, g_low, w_gate_up, b_gate, norm_g):
    s = qk.shape[0]
    kw = w_gate_up.shape[1]
    vw = v.shape[1]
    dk, dv = kw // GLA_HEADS, vw // GLA_HEADS
    rows = _tile(s, 256)
    pair_rows = 2 * GLA_CHUNK
    pairs = rows // pair_rows
    t = jnp.arange(pair_rows)
    tril = jnp.logical_and(t[:, None] >= t[None, :],
                           t[:, None] // GLA_CHUNK == t[None, :] // GLA_CHUNK).astype(BF16)
    return pl.pallas_call(
        functools.partial(_gla_kernel, pairs=pairs, dk=dk, dv=dv),
        grid=(s // rows,),
        in_specs=[pl.BlockSpec((rows, 2 * kw), lambda i: (i, 0)),
                  pl.BlockSpec((rows, vw), lambda i: (i, 0)),
                  pl.BlockSpec((rows, vw), lambda i: (i, 0)),
                  pl.BlockSpec((rows, V7X_LANES), lambda i: (i, 0)),
                  pl.BlockSpec((V7X_LANES, kw), lambda i: (0, 0)),
                  pl.BlockSpec((1, kw), lambda i: (0, 0)),
                  pl.BlockSpec((1, dv), lambda i: (0, 0)),
                  pl.BlockSpec((pair_rows, pair_rows), lambda i: (0, 0))],
        out_specs=pl.BlockSpec((rows, vw), lambda i: (i, 0)),
        out_shape=jax.ShapeDtypeStruct((s, vw), BF16),
        scratch_shapes=[pltpu.VMEM((GLA_HEADS, dk, dv), F32)],
        compiler_params=_params("arbitrary"),
        name="gla",
    )(qk, v, r, g_low, w_gate_up, b_gate.reshape(1, kw), norm_g.reshape(1, dv), tril)


def _sb_kernel(q_ref, k_ref, v_ref, t_ref, o_ref, acc_ref, run_ref, *, tq, tk, heads):
    qi = pl.program_id(1)
    tmat = t_ref[...]
    row_groups = tq // tk
    base = qi * row_groups
    below_diag = (lax.broadcasted_iota(jnp.int32, (tk, tk), 1)
                  < lax.broadcasted_iota(jnp.int32, (tk, tk), 0))

    chains = heads * row_groups
    all_chains = range(chains)
    group = [c % row_groups for c in all_chains]
    rows = [slice(g * tk, (g + 1) * tk) for g in group]
    lanes = [slice((c // row_groups) * SB_HEAD_DIM, (c // row_groups + 1) * SB_HEAD_DIM)
             for c in all_chains]

    def sweep(blocks, diagonal, live):
        starts = [pl.multiple_of(blocks[g] * tk, tk) for g in group]
        if live is not None:
            live = [live[g] for g in group]
        zs = [_dot_nt(q_ref[rows[c], lanes[c]], k_ref[pl.ds(starts[c], tk), lanes[c]])
              for c in all_chains]
        log_betas, halves = [], []
        for z in zs:
            softplus = jnp.maximum(z, 0.0) + jnp.log(1.0 + jnp.exp(-jnp.abs(z)))
            log_betas.append(z - softplus)
            neg_log_rest = softplus
            if diagonal:
                neg_log_rest = jnp.where(below_diag, neg_log_rest, 0.0)
            halves.append(jnp.concatenate(_split_bf16(neg_log_rest), axis=1))
        cms = [_dot(hl, tmat) for hl in halves]
        atts, runs = [], []
        for c in all_chains:
            log_att = log_betas[c] + cms[c][:, :tk]
            if diagonal:
                atts.append(jnp.where(below_diag, jnp.exp(log_att), 0.0))
                runs.append(cms[c][:, tk:])
            else:
                run = jnp.where(live[c], run_ref[rows[c], lanes[c]], -1e30)
                atts.append(jnp.exp(log_att + run))
                runs.append(run + cms[c][:, tk:])
        pvs = [_dot(atts[c].astype(BF16), v_ref[pl.ds(starts[c], tk), lanes[c]])
               for c in all_chains]
        if not diagonal:
            pvs = [acc_ref[rows[c], lanes[c]] + pvs[c] for c in all_chains]
        for c in all_chains:
            acc_ref[rows[c], lanes[c]] = pvs[c]
            run_ref[rows[c], lanes[c]] = runs[c]
        if diagonal:
            return jnp.float32(0.0)
        return jnp.max(functools.reduce(jnp.maximum, runs))

    top = sweep([base + g for g in range(row_groups)], True, None)

    def cond(st):
        n, top = st
        return jnp.logical_and(n <= base + row_groups - 1, top > F32_EXP_UNDERFLOW)

    def body(st):
        n, _ = st
        blocks = [base + g - n for g in range(row_groups)]
        top = sweep([jnp.maximum(j, 0) for j in blocks], False, [j >= 0 for j in blocks])
        return n + 1, top

    lax.while_loop(cond, body, (jnp.int32(1), top))
    o_ref[...] = acc_ref[...].astype(o_ref.dtype)


def _sb_attention(q, k, v):
    s, d = q.shape
    dh = SB_HEAD_DIM
    tk = V7X_LANES
    tq = _tile(s, 1024)
    r = lax.broadcasted_iota(jnp.int32, (tk, 2 * tk), 0)
    c = lax.broadcasted_iota(jnp.int32, (tk, 2 * tk), 1)
    half = jnp.where(jnp.logical_or(c >= tk, r > c), -1.0, 0.0).astype(BF16)
    tmat = jnp.concatenate([half, half], axis=0)
    heads = SB_HEADS_PER_STEP
    wh = heads * dh
    return pl.pallas_call(
        functools.partial(_sb_kernel, tq=tq, tk=tk, heads=heads),
        grid=(d // wh, s // tq),
        in_specs=[pl.BlockSpec((tq, wh), lambda h, i: (i, h)),
                  pl.BlockSpec((s, wh), lambda h, i: (0, h)),
                  pl.BlockSpec((s, wh), lambda h, i: (0, h)),
                  pl.BlockSpec((2 * tk, 2 * tk), lambda h, i: (0, 0))],
        out_specs=pl.BlockSpec((tq, wh), lambda h, i: (i, h)),
        out_shape=jax.ShapeDtypeStruct((s, d), BF16),
        scratch_shapes=[pltpu.VMEM((tq, wh), F32), pltpu.VMEM((tq, wh), F32)],
        compiler_params=_params("arbitrary", "arbitrary"),
        name="sb_attention",
    )(q, k, v, tmat)


def kernel(x, c, w_ada, b_ada, ada_table, gla_w_in, gla_w_gate_up, gla_b_gate, gla_norm_g, gla_w_out, kv_norm_g, w_kv, k_norm_g, sb_w_q, sb_q_norm_g, sb_w_out, ffn_w_in, ffn_conv_w, ffn_conv_b, ffn_w_out):
    batch, s, d = x.shape
    assert batch == 1, "adaLN modulation rows are built for a single sequence"
    depth = ada_table.shape[0]
    n_a = gla_w_in.shape[0]
    kw = gla_w_gate_up.shape[2]
    rank = gla_w_gate_up.shape[1]
    x = x.reshape(s, d)

    w_up_bf = jnp.pad(gla_w_gate_up, ((0, 0), (0, V7X_LANES - rank), (0, 0))).astype(BF16)
    w_kv3 = w_kv.reshape(1, d, 2 * d)

    def mixer_weights(l):
        if l >= depth:
            return []
        if l < n_a:
            return [(gla_w_out, l)]
        own = [(sb_w_q, l - n_a), (sb_w_out, l - n_a)]
        return ([(w_kv3, 0)] if l == n_a else []) + own

    gla_w_in_bf = gla_w_in.astype(BF16)
    mixer_bf = [w[i].astype(BF16) for w, i in mixer_weights(0)]

    mods = _cond(c, w_ada, b_ada, ada_table)
    k_sh = v_sh = None
    for l in range(depth):
        mod = mods[l]
        shift, scale, gate = (mod[i * d:(i + 1) * d] for i in (0, 1, 2))
        h = _norm_affine(x, scale, shift, True)
        if l < n_a:
            (w_out_bf,) = mixer_bf
            qk = _mm_plain(h, gla_w_in_bf, F32, layer=l, col0=0, n=2 * kw)
            v = _mm_plain(h, gla_w_in_bf, BF16, layer=l, col0=2 * kw, n=d)
            r = _mm_plain(h, gla_w_in_bf, F32, layer=l, col0=2 * kw + d, n=d)
            g_low = _mm_plain(h, gla_w_in_bf, F32, layer=l, col0=2 * kw + 2 * d,
                              n=V7X_LANES, valid_cols=rank)
            o = _gla(qk, v, r, g_low, w_up_bf[l], gla_b_gate[l], gla_norm_g[l])
        else:
            if l == n_a:
                w_kv_bf = mixer_bf.pop(0)
                src = _norm_affine(x, kv_norm_g, jnp.zeros_like(kv_norm_g), False)
                k_sh = _mm_headnorm(src, w_kv_bf, k_norm_g, 1.0, col0=0, n=d)
                v_sh = _mm_plain(src, w_kv_bf, BF16, col0=d, n=d)
            w_q_bf, w_out_bf = mixer_bf
            q = _mm_headnorm(h, w_q_bf, sb_q_norm_g[l - n_a], SB_HEAD_DIM ** -0.5)
            o = _sb_attention(q, k_sh, v_sh)
        x = _mm_residual(o, w_out_bf, x, gate)

        shift, scale, gate = (mod[i * d:(i + 1) * d] for i in (3, 4, 5))
        h = _norm_affine(x, scale, shift, True)
        g, rounded = _ffn_in(h, ffn_w_in, ffn_conv_w, ffn_conv_b, l,
                             [(ffn_w_out, l)] + mixer_weights(l + 1))
        x = _mm_residual(g, rounded[0], x, gate)
        mixer_bf = rounded[1:]
    return x.reshape(batch, s, d)
```

```python
import functools

import jax
import jax.numpy as jnp
from jax import lax
from jax.experimental import pallas as pl
from jax.experimental.pallas import tpu as pltpu

F32 = jnp.float32
BF16 = jnp.bfloat16

EPS = 1e-6
N_MOD = 6
CONV_WIDTH = 3
GLA_CHUNK = 64
GLA_HEADS = 4
GLA_TAU = 16.0
SB_HEAD_DIM = 128
SB_HEADS_PER_STEP = 2

V7X_LANES = 128
V7X_SUBLANES = 8
V7X_VMEM_LIMIT_BYTES = 56 * 1024 * 1024

F32_EXP_UNDERFLOW = -88.0


def _params(*semantics):
    return pltpu.CompilerParams(dimension_semantics=semantics,
                                vmem_limit_bytes=V7X_VMEM_LIMIT_BYTES)


def _tile(n, pref):
    if n <= pref:
        return n
    t = (pref // V7X_LANES) * V7X_LANES
    while t >= V7X_LANES:
        if n % t == 0:
            return t
        t -= V7X_LANES
    return n


def _dot(a, b):
    return jnp.dot(a, b, preferred_element_type=F32)


def _dot_nt(a, b):
    return lax.dot_general(a, b, (((1,), (1,)), ((), ())), preferred_element_type=F32)


def _dot_tn(a, b):
    return lax.dot_general(a, b, (((0,), (0,)), ((), ())), preferred_element_type=F32)


def _silu(x):
    return x / (1.0 + jnp.exp(-x))


def _split_bf16(x):
    hi = x.astype(BF16)
    lo = (x - hi.astype(F32)).astype(BF16)
    return hi, lo


def _cond_kernel(c_ref, w_ref, b_ref, tab_ref, o_ref):
    s = _silu(c_ref[...]).astype(BF16)
    acc = _dot(s, w_ref[...].astype(BF16))
    o_ref[...] = acc[0:1, :] + b_ref[...] + tab_ref[...]


def _cond(c, w_ada, b_ada, ada_table):
    d = c.shape[1]
    depth = ada_table.shape[0]
    n = w_ada.shape[1]
    tn = _tile(n, 512)
    c8 = jnp.broadcast_to(c, (V7X_SUBLANES, d))
    return pl.pallas_call(
        _cond_kernel,
        grid=(n // tn,),
        in_specs=[pl.BlockSpec((V7X_SUBLANES, d), lambda j: (0, 0)),
                  pl.BlockSpec((d, tn), lambda j: (0, j)),
                  pl.BlockSpec((1, tn), lambda j: (0, j)),
                  pl.BlockSpec((depth, tn), lambda j: (0, j))],
        out_specs=pl.BlockSpec((depth, tn), lambda j: (0, j)),
        out_shape=jax.ShapeDtypeStruct((depth, n), F32),
        compiler_params=_params("arbitrary"),
        name="cond",
    )(c8, w_ada, b_ada.reshape(1, n), ada_table.reshape(depth, n))


def _norm_affine_kernel(x_ref, mul_ref, add_ref, o_ref, *, plus_one):
    x = x_ref[...]
    ms = jnp.mean(x * x, axis=-1, keepdims=True)
    xn = x * lax.rsqrt(ms + EPS)
    mul = mul_ref[...]
    if plus_one:
        mul = 1.0 + mul
    o_ref[...] = (xn * mul + add_ref[...]).astype(o_ref.dtype)


def _norm_affine(x, mul, add, plus_one):
    s, d = x.shape
    tm = _tile(s, 256)
    return pl.pallas_call(
        functools.partial(_norm_affine_kernel, plus_one=plus_one),
        grid=(s // tm,),
        in_specs=[pl.BlockSpec((tm, d), lambda i: (i, 0)),
                  pl.BlockSpec((1, d), lambda i: (0, 0)),
                  pl.BlockSpec((1, d), lambda i: (0, 0))],
        out_specs=pl.BlockSpec((tm, d), lambda i: (i, 0)),
        out_shape=jax.ShapeDtypeStruct((s, d), BF16),
        compiler_params=_params("arbitrary"),
        name="norm_affine",
    )(x, mul.reshape(1, d), add.reshape(1, d))


def _mm_plain_kernel(a_ref, w_ref, o_ref, *, valid_cols=None):
    w = w_ref[...]
    if valid_cols is not None:
        col = lax.broadcasted_iota(jnp.int32, w.shape, 1)
        w = jnp.where(col < valid_cols, w, jnp.zeros_like(w))
    o_ref[...] = _dot(a_ref[...], w).astype(o_ref.dtype)


def _mm_headnorm_kernel(a_ref, w_ref, g_ref, o_ref, *, post_scale):
    acc = _dot(a_ref[...], w_ref[...])
    g = g_ref[...] * post_scale
    for c in range(acc.shape[1] // SB_HEAD_DIM):
        sl = slice(c * SB_HEAD_DIM, (c + 1) * SB_HEAD_DIM)
        blk = acc[:, sl]
        ms = jnp.mean(blk * blk, axis=-1, keepdims=True)
        o_ref[:, sl] = (blk * lax.rsqrt(ms + EPS) * g).astype(o_ref.dtype)


def _mm_residual_kernel(a_ref, *refs):
    w_refs, (x_ref, gate_ref, o_ref) = refs[:-3], refs[-3:]
    kb = w_refs[0].shape[0]
    acc = _dot(a_ref[:, 0:kb], w_refs[0][...])
    for b in range(1, len(w_refs)):
        acc = acc + _dot(a_ref[:, b * kb:(b + 1) * kb], w_refs[b][...])
    o_ref[...] = x_ref[...] + gate_ref[...] * acc


def _mm_tiles(m, n, k):
    tm = _tile(m, 1024)
    tn = _tile(n, 1024 if k <= 4096 else 512)
    if k > 4096:
        tm = _tile(m, 512)
    return tm, tn


def _weight_spec(w, layer, col0, tn):
    k = w.shape[-2]
    assert col0 % tn == 0
    c0 = col0 // tn
    if w.ndim == 2:
        return pl.BlockSpec((k, tn), lambda i, j: (0, c0 + j))
    return pl.BlockSpec((None, k, tn), lambda i, j: (layer, 0, c0 + j))


def _mm_plain(a, w, out_dtype, *, layer=0, col0=0, n=None, valid_cols=None):
    m, k = a.shape
    n = w.shape[-1] if n is None else n
    tm, tn = _mm_tiles(m, n, k)
    return pl.pallas_call(
        functools.partial(_mm_plain_kernel, valid_cols=valid_cols),
        grid=(m // tm, n // tn),
        in_specs=[pl.BlockSpec((tm, k), lambda i, j: (i, 0)),
                  _weight_spec(w, layer, col0, tn)],
        out_specs=pl.BlockSpec((tm, tn), lambda i, j: (i, j)),
        out_shape=jax.ShapeDtypeStruct((m, n), out_dtype),
        compiler_params=_params("arbitrary", "arbitrary"),
        name="mm_plain",
    )(a, w)


def _mm_headnorm(a, w, g, post_scale, *, layer=0, col0=0, n=None):
    m, k = a.shape
    n = w.shape[-1] if n is None else n
    tm, tn = _mm_tiles(m, n, k)
    return pl.pallas_call(
        functools.partial(_mm_headnorm_kernel, post_scale=post_scale),
        grid=(m // tm, n // tn),
        in_specs=[pl.BlockSpec((tm, k), lambda i, j: (i, 0)),
                  _weight_spec(w, layer, col0, tn),
                  pl.BlockSpec((1, SB_HEAD_DIM), lambda i, j: (0, 0))],
        out_specs=pl.BlockSpec((tm, tn), lambda i, j: (i, j)),
        out_shape=jax.ShapeDtypeStruct((m, n), BF16),
        compiler_params=_params("arbitrary", "arbitrary"),
        name="mm_headnorm",
    )(a, w, g.reshape(1, SB_HEAD_DIM))


def _mm_residual(a, w, x, gate):
    m, k = a.shape
    n = w.shape[-1]
    tm, tn = _mm_tiles(m, n, k)
    bands = 2 if k > 4096 and k % (2 * V7X_LANES) == 0 else 1
    kb = k // bands
    w_specs = [pl.BlockSpec((kb, tn), lambda i, j, b=b: (b, j)) for b in range(bands)]
    return pl.pallas_call(
        _mm_residual_kernel,
        grid=(m // tm, n // tn),
        in_specs=[pl.BlockSpec((tm, k), lambda i, j: (i, 0))] + w_specs + [
                  pl.BlockSpec((tm, tn), lambda i, j: (i, j)),
                  pl.BlockSpec((1, tn), lambda i, j: (0, j))],
        out_specs=pl.BlockSpec((tm, tn), lambda i, j: (i, j)),
        out_shape=jax.ShapeDtypeStruct((m, n), F32),
        compiler_params=_params("arbitrary", "arbitrary"),
        name="mm_residual",
    )(a, *([w] * bands), x, gate.reshape(1, n))


FFN_CHUNK_ROWS = 128
FFN_BAND_ROWS = 512


def _ffn_in_kernel(*refs, n_h, n_side):
    h_refs, refs = refs[:n_h], refs[n_h:]
    (wa_ref, wb_ref, cwa_ref, cwb_ref, cba_ref, cbb_ref), refs = refs[:6], refs[6:]
    side_f32_refs = refs[:n_side]
    o_ref = refs[n_side]
    side_bf_refs = refs[n_side + 1:2 * n_side + 1]
    wa_bf_ref, wb_bf_ref, carry_a_ref, carry_b_ref, ua_ref, ub_ref = refs[2 * n_side + 1:]
    hm = h_refs[0].shape[0]
    tm = n_h * hm
    cm = min(FFN_CHUNK_ROWS, hm)

    for src_ref, dst_ref in zip(side_f32_refs, side_bf_refs):
        dst_ref[...] = src_ref[...].astype(BF16)

    @pl.when(pl.program_id(1) == 0)
    def _():
        wa_bf_ref[...] = wa_ref[...].astype(BF16)
        wb_bf_ref[...] = wb_ref[...].astype(BF16)
        carry_a_ref[...] = jnp.zeros_like(carry_a_ref)
        carry_b_ref[...] = jnp.zeros_like(carry_b_ref)

    row = lax.broadcasted_iota(jnp.int32, carry_a_ref.shape, 0)

    def conv(u, prev, cw_ref, cb_ref):
        w0, w1, w2 = cw_ref[0:1, :], cw_ref[1:2, :], cw_ref[2:3, :]
        bias = cb_ref[...]
        u1 = pltpu.roll(u, 1, axis=0)
        u2 = pltpu.roll(u, 2, axis=0)
        head1 = jnp.where(row < 1, pltpu.roll(prev, 1, axis=0), u1[0:V7X_SUBLANES])
        head2 = jnp.where(row < 2, pltpu.roll(prev, 2, axis=0), u2[0:V7X_SUBLANES])
        head = bias + head2 * w0 + head1 * w1 + u[0:V7X_SUBLANES] * w2
        tail = (bias + u2[V7X_SUBLANES:] * w0 + u1[V7X_SUBLANES:] * w1
                + u[V7X_SUBLANES:] * w2)
        return head, tail

    prev_a = carry_a_ref[...]
    prev_b = carry_b_ref[...]
    for c in range(tm // cm):
        r0 = c * cm
        h = h_refs[r0 // hm][r0 % hm:r0 % hm + cm, :]
        slot = lax.rem(pl.program_id(1) + c, 2)
        ua_ref[slot] = _dot(h, wa_bf_ref[...])
        ub_ref[slot] = _dot(h, wb_bf_ref[...])
        ua = ua_ref[slot]
        ub = ub_ref[slot]
        a_head, a_tail = conv(ua, prev_a, cwa_ref, cba_ref)
        b_head, b_tail = conv(ub, prev_b, cwb_ref, cbb_ref)
        o_ref[r0:r0 + V7X_SUBLANES, :] = (_silu(a_head) * b_head).astype(o_ref.dtype)
        o_ref[r0 + V7X_SUBLANES:r0 + cm, :] = (_silu(a_tail) * b_tail).astype(o_ref.dtype)
        prev_a = ua[cm - V7X_SUBLANES:cm]
        prev_b = ub[cm - V7X_SUBLANES:cm]
    carry_a_ref[...] = prev_a
    carry_b_ref[...] = prev_b


def _slab_rows(rows, steps):
    packed = 2 * V7X_SUBLANES
    return min(m for m in range(packed, rows + 1, packed) if rows % m == 0 and m * steps >= rows)


def _ffn_in(h, w_in, conv_w, conv_b, layer, to_round):
    s, d = h.shape
    f = w_in.shape[-1] // 2
    tm = _tile(s, 1024)
    hm = _tile(tm, FFN_BAND_ROWS)
    n_h = tm // hm
    tn = _tile(f, 256)
    nj = f // tn
    ni = s // tm
    cm = min(FFN_CHUNK_ROWS, hm)
    conv_b = conv_b.reshape(conv_b.shape[0], 1, 2 * f)
    h_specs = [pl.BlockSpec((hm, d), lambda j, i, b=b: (i * n_h + b, 0)) for b in range(n_h)]

    side_in_specs, side_out_specs, side_out_shapes = [], [], []
    for w, w_layer in to_round:
        rows, cols = w.shape[-2:]
        slab = _slab_rows(rows, nj * ni)
        last = rows // slab - 1

        def slab_index(j, i, last=last):
            return jnp.minimum(j * ni + i, last)

        side_in_specs.append(pl.BlockSpec(
            (None, slab, cols), lambda j, i, w_layer=w_layer, f=slab_index: (w_layer, f(j, i), 0)))
        side_out_specs.append(pl.BlockSpec((slab, cols), lambda j, i, f=slab_index: (f(j, i), 0)))
        side_out_shapes.append(jax.ShapeDtypeStruct((rows, cols), BF16))

    outs = pl.pallas_call(
        functools.partial(_ffn_in_kernel, n_h=n_h, n_side=len(to_round)),
        grid=(nj, ni),
        in_specs=h_specs + [
                  pl.BlockSpec((None, d, tn), lambda j, i: (layer, 0, j)),
                  pl.BlockSpec((None, d, tn), lambda j, i: (layer, 0, j + nj)),
                  pl.BlockSpec((None, CONV_WIDTH, tn), lambda j, i: (layer, 0, j)),
                  pl.BlockSpec((None, CONV_WIDTH, tn), lambda j, i: (layer, 0, j + nj)),
                  pl.BlockSpec((None, 1, tn), lambda j, i: (layer, 0, j)),
                  pl.BlockSpec((None, 1, tn), lambda j, i: (layer, 0, j + nj))] + side_in_specs,
        out_specs=[pl.BlockSpec((tm, tn), lambda j, i: (i, j))] + side_out_specs,
        out_shape=[jax.ShapeDtypeStruct((s, f), BF16)] + side_out_shapes,
        scratch_shapes=[pltpu.VMEM((d, tn), BF16),
                        pltpu.VMEM((d, tn), BF16),
                        pltpu.VMEM((V7X_SUBLANES, tn), F32),
                        pltpu.VMEM((V7X_SUBLANES, tn), F32),
                        pltpu.VMEM((2, cm, tn), F32),
                        pltpu.VMEM((2, cm, tn), F32)],
        compiler_params=_params("arbitrary", "arbitrary"),
        name="ffn_in",
    )(*([h] * n_h), w_in, w_in, conv_w, conv_w, conv_b, conv_b, *(w for w, _ in to_round))
    return outs[0], list(outs[1:])


def _gla_kernel(qk_ref, v_ref, r_ref, gl_ref, wgu_ref, bg_ref, ng_ref, tril_ref,
                o_ref, state_ref, *, pairs, dk, dv):
    kw = GLA_HEADS * dk
    pair_rows = 2 * GLA_CHUNK

    @pl.when(pl.program_id(0) == 0)
    def _():
        state_ref[...] = jnp.zeros_like(state_ref)

    tril = tril_ref[...]
    own_causal = tril > 0
    r_id = lax.broadcasted_iota(jnp.int32, (pair_rows, pair_rows), 0)
    c_id = lax.broadcasted_iota(jnp.int32, (pair_rows, pair_rows), 1)
    b_sees_a = jnp.logical_and(r_id >= GLA_CHUNK, c_id < GLA_CHUNK)
    in_b = lax.broadcasted_iota(jnp.int32, (pair_rows, 1), 0) >= GLA_CHUNK
    q_scale = dk ** -0.5

    def pair(p, carry):
        rows = pl.ds(pl.multiple_of(p * pair_rows, pair_rows), pair_rows)
        y = _dot(gl_ref[rows, :].astype(BF16), wgu_ref[...]) + bg_ref[...]
        log_alpha = (jnp.minimum(y, 0.0) - jnp.log(1.0 + jnp.exp(-jnp.abs(y)))) * (1.0 / GLA_TAU)
        hi, lo = _split_bf16(log_alpha)
        b_cum = _dot(tril, hi) + _dot(tril, lo)
        b_last_a = b_cum[GLA_CHUNK - 1:GLA_CHUNK, :]
        b_last_b = b_cum[pair_rows - 1:pair_rows, :]
        d_a = jnp.exp(b_last_a)
        d_b = jnp.exp(b_last_b)
        e_pos = jnp.exp(b_cum)
        e_neg = jnp.exp(-b_cum)
        e_end = jnp.exp(jnp.where(in_b, b_last_b, b_last_a) - b_cum)
        e_pos_s = e_pos * jnp.where(in_b, d_a, 1.0)
        e_end_s = e_end * jnp.where(in_b, 1.0, d_b)
        decay_rows = jnp.broadcast_to(d_a * d_b, (V7X_LANES, kw))
        for h in range(GLA_HEADS):
            ksl = slice(h * dk, (h + 1) * dk)
            vsl = slice(h * dv, (h + 1) * dv)
            q = qk_ref[rows, ksl] * q_scale
            k = qk_ref[rows, kw + h * dk:kw + (h + 1) * dk]
            q_dec = (q * e_pos[:, ksl]).astype(BF16)
            k_dec = (k * e_neg[:, ksl]).astype(BF16)
            k_end = (k * e_end[:, ksl]).astype(BF16)
            q_dec_s = (q * e_pos_s[:, ksl]).astype(BF16)
            k_end_s = (k * e_end_s[:, ksl]).astype(BF16)
            v = v_ref[rows, vsl]
            scores = jnp.where(own_causal, _dot_nt(q_dec, k_dec),
                               jnp.where(b_sees_a, _dot_nt(q_dec, k_end), 0.0)).astype(BF16)
            state = state_ref[h]
            o = _dot(scores, v) + _dot(q_dec_s, state.astype(BF16))
            decay = jnp.concatenate(
                [decay_rows[:, h * dk + b * V7X_LANES:h * dk + (b + 1) * V7X_LANES].T
                 for b in range(dk // V7X_LANES)], axis=0)
            decay = jnp.concatenate([decay] * (dv // V7X_LANES), axis=1)
            state_ref[h] = state * decay + _dot_tn(k_end_s, v)
            ms = jnp.mean(o * o, axis=-1, keepdims=True)
            o = o * lax.rsqrt(ms + EPS) * ng_ref[...]
            o_ref[rows, vsl] = (_silu(r_ref[rows, vsl]) * o).astype(o_ref.dtype)
        return carry

    lax.fori_loop(0, pairs, pair, 0)


def _gla(qk, v, r, g_low, w_gate_up, b_gate, norm_g):
    s = qk.shape[0]
    kw = w_gate_up.shape[1]
    vw = v.shape[1]
    dk, dv = kw // GLA_HEADS, vw // GLA_HEADS
    rows = _tile(s, 256)
    pair_rows = 2 * GLA_CHUNK
    pairs = rows // pair_rows
    t = jnp.arange(pair_rows)
    tril = jnp.logical_and(t[:, None] >= t[None, :],
                           t[:, None] // GLA_CHUNK == t[None, :] // GLA_CHUNK).astype(BF16)
    return pl.pallas_call(
        functools.partial(_gla_kernel, pairs=pairs, dk=dk, dv=dv),
        grid=(s // rows,),
        in_specs=[pl.BlockSpec((rows, 2 * kw), lambda i: (i, 0)),
                  pl.BlockSpec((rows, vw), lambda i: (i, 0)),
                  pl.BlockSpec((rows, vw), lambda i: (i, 0)),
                  pl.BlockSpec((rows, V7X_LANES), lambda i: (i, 0)),
                  pl.BlockSpec((V7X_LANES, kw), lambda i: (0, 0)),
                  pl.BlockSpec((1, kw), lambda i: (0, 0)),
                  pl.BlockSpec((1, dv), lambda i: (0, 0)),
                  pl.BlockSpec((pair_rows, pair_rows), lambda i: (0, 0))],
        out_specs=pl.BlockSpec((rows, vw), lambda i: (i, 0)),
        out_shape=jax.ShapeDtypeStruct((s, vw), BF16),
        scratch_shapes=[pltpu.VMEM((GLA_HEADS, dk, dv), F32)],
        compiler_params=_params("arbitrary"),
        name="gla",
    )(qk, v, r, g_low, w_gate_up, b_gate.reshape(1, kw), norm_g.reshape(1, dv), tril)


def _sb_kernel(q_ref, k_ref, v_ref, t_ref, o_ref, acc_ref, run_ref, *, tq, tk, heads):
    qi = pl.program_id(1)
    tmat = t_ref[...]
    row_groups = tq // tk
    base = qi * row_groups
    below_diag = (lax.broadcasted_iota(jnp.int32, (tk, tk), 1)
                  < lax.broadcasted_iota(jnp.int32, (tk, tk), 0))

    chains = heads * row_groups
    all_chains = range(chains)
    group = [c % row_groups for c in all_chains]
    rows = [slice(g * tk, (g + 1) * tk) for g in group]
    lanes = [slice((c // row_groups) * SB_HEAD_DIM, (c // row_groups + 1) * SB_HEAD_DIM)
             for c in all_chains]

    def sweep(blocks, diagonal, live):
        starts = [pl.multiple_of(blocks[g] * tk, tk) for g in group]
        if live is not None:
            live = [live[g] for g in group]
        zs = [_dot_nt(q_ref[rows[c], lanes[c]], k_ref[pl.ds(starts[c], tk), lanes[c]])
              for c in all_chains]
        log_betas, halves = [], []
        for z in zs:
            softplus = jnp.maximum(z, 0.0) + jnp.log(1.0 + jnp.exp(-jnp.abs(z)))
            log_betas.append(z - softplus)
            neg_log_rest = softplus
            if diagonal:
                neg_log_rest = jnp.where(below_diag, neg_log_rest, 0.0)
            halves.append(jnp.concatenate(_split_bf16(neg_log_rest), axis=1))
        cms = [_dot(hl, tmat) for hl in halves]
        atts, runs = [], []
        for c in all_chains:
            log_att = log_betas[c] + cms[c][:, :tk]
            if diagonal:
                atts.append(jnp.where(below_diag, jnp.exp(log_att), 0.0))
                runs.append(cms[c][:, tk:])
            else:
                run = jnp.where(live[c], run_ref[rows[c], lanes[c]], -1e30)
                atts.append(jnp.exp(log_att + run))
                runs.append(run + cms[c][:, tk:])
        pvs = [_dot(atts[c].astype(BF16), v_ref[pl.ds(starts[c], tk), lanes[c]])
               for c in all_chains]
        if not diagonal:
            pvs = [acc_ref[rows[c], lanes[c]] + pvs[c] for c in all_chains]
        for c in all_chains:
            acc_ref[rows[c], lanes[c]] = pvs[c]
            run_ref[rows[c], lanes[c]] = runs[c]
        if diagonal:
            return jnp.float32(0.0)
        return jnp.max(functools.reduce(jnp.maximum, runs))

    top = sweep([base + g for g in range(row_groups)], True, None)

    def cond(st):
        n, top = st
        return jnp.logical_and(n <= base + row_groups - 1, top > F32_EXP_UNDERFLOW)

    def body(st):
        n, _ = st
        blocks = [base + g - n for g in range(row_groups)]
        top = sweep([jnp.maximum(j, 0) for j in blocks], False, [j >= 0 for j in blocks])
        return n + 1, top

    lax.while_loop(cond, body, (jnp.int32(1), top))
    o_ref[...] = acc_ref[...].astype(o_ref.dtype)


def _sb_attention(q, k, v):
    s, d = q.shape
    dh = SB_HEAD_DIM
    tk = V7X_LANES
    tq = _tile(s, 1024)
    r = lax.broadcasted_iota(jnp.int32, (tk, 2 * tk), 0)
    c = lax.broadcasted_iota(jnp.int32, (tk, 2 * tk), 1)
    half = jnp.where(jnp.logical_or(c >= tk, r > c), -1.0, 0.0).astype(BF16)
    tmat = jnp.concatenate([half, half], axis=0)
    heads = SB_HEADS_PER_STEP
    wh = heads * dh
    return pl.pallas_call(
        functools.partial(_sb_kernel, tq=tq, tk=tk, heads=heads),
        grid=(d // wh, s // tq),
        in_specs=[pl.BlockSpec((tq, wh), lambda h, i: (i, h)),
                  pl.BlockSpec((s, wh), lambda h, i: (0, h)),
                  pl.BlockSpec((s, wh), lambda h, i: (0, h)),
                  pl.BlockSpec((2 * tk, 2 * tk), lambda h, i: (0, 0))],
        out_specs=pl.BlockSpec((tq, wh), lambda h, i: (i, h)),
        out_shape=jax.ShapeDtypeStruct((s, d), BF16),
        scratch_shapes=[pltpu.VMEM((tq, wh), F32), pltpu.VMEM((tq, wh), F32)],
        compiler_params=_params("arbitrary", "arbitrary"),
        name="sb_attention",
    )(q, k, v, tmat)


def kernel(x, c, w_ada, b_ada, ada_table, gla_w_in, gla_w_gate_up, gla_b_gate, gla_norm_g, gla_w_out, kv_norm_g, w_kv, k_norm_g, sb_w_q, sb_q_norm_g, sb_w_out, ffn_w_in, ffn_conv_w, ffn_conv_b, ffn_w_out):
    batch, s, d = x.shape
    assert batch == 1, "adaLN modulation rows are built for a single sequence"
    depth = ada_table.shape[0]
    n_a = gla_w_in.shape[0]
    kw = gla_w_gate_up.shape[2]
    rank = gla_w_gate_up.shape[1]
    x = x.reshape(s, d)

    w_up_bf = jnp.pad(gla_w_gate_up, ((0, 0), (0, V7X_LANES - rank), (0, 0))).astype(BF16)
    w_kv3 = w_kv.reshape(1, d, 2 * d)

    def mixer_weights(l):
        if l >= depth:
            return []
        if l < n_a:
            return [(gla_w_out, l)]
        own = [(sb_w_q, l - n_a), (sb_w_out, l - n_a)]
        return ([(w_kv3, 0)] if l == n_a else []) + own

    gla_w_in_bf = gla_w_in.astype(BF16)
    mixer_bf = [w[i].astype(BF16) for w, i in mixer_weights(0)]

    mods = _cond(c, w_ada, b_ada, ada_table)
    k_sh = v_sh = None
    for l in range(depth):
        mod = mods[l]
        shift, scale, gate = (mod[i * d:(i + 1) * d] for i in (0, 1, 2))
        h = _norm_affine(x, scale, shift, True)
        if l < n_a:
            (w_out_bf,) = mixer_bf
            qk = _mm_plain(h, gla_w_in_bf, F32, layer=l, col0=0, n=2 * kw)
            v = _mm_plain(h, gla_w_in_bf, BF16, layer=l, col0=2 * kw, n=d)
            r = _mm_plain(h, gla_w_in_bf, F32, layer=l, col0=2 * kw + d, n=d)
            g_low = _mm_plain(h, gla_w_in_bf, F32, layer=l, col0=2 * kw + 2 * d,
                              n=V7X_LANES, valid_cols=rank)
            o = _gla(qk, v, r, g_low, w_up_bf[l], gla_b_gate[l], gla_norm_g[l])
        else:
            if l == n_a:
                w_kv_bf = mixer_bf.pop(0)
                src = _norm_affine(x, kv_norm_g, jnp.zeros_like(kv_norm_g), False)
                k_sh = _mm_headnorm(src, w_kv_bf, k_norm_g, 1.0, col0=0, n=d)
                v_sh = _mm_plain(src, w_kv_bf, BF16, col0=d, n=d)
            w_q_bf, w_out_bf = mixer_bf
            q = _mm_headnorm(h, w_q_bf, sb_q_norm_g[l - n_a], SB_HEAD_DIM ** -0.5)
            o = _sb_attention(q, k_sh, v_sh)
        x = _mm_residual(o, w_out_bf, x, gate)

        shift, scale, gate = (mod[i * d:(i + 1) * d] for i in (3, 4, 5))
        h = _norm_affine(x, scale, shift, True)
        g, rounded = _ffn_in(h, ffn_w_in, ffn_conv_w, ffn_conv_b, l,
                             [(ffn_w_out, l)] + mixer_weights(l + 1))
        x = _mm_residual(g, rounded[0], x, gate)
        mixer_bf = rounded[1:]
    return x.reshape(batch, s, d)
```

```python
import functools

import jax
import jax.numpy as jnp
from jax import lax
from jax.experimental import pallas as pl
from jax.experimental.pallas import tpu as pltpu

F32 = jnp.float32
BF16 = jnp.bfloat16

EPS = 1e-6
CONV_WIDTH = 3
GLA_CHUNK = 64
GLA_HEADS = 4
GLA_TAU = 16.0
SB_HEAD_DIM = 128
SB_HEADS_PER_STEP = 2

V7X_LANES = 128
V7X_SUBLANES = 8
V7X_VMEM_LIMIT_BYTES = 56 * 1024 * 1024

F32_EXP_UNDERFLOW = -88.0


def _params(*semantics):
    return pltpu.CompilerParams(dimension_semantics=semantics,
                                vmem_limit_bytes=V7X_VMEM_LIMIT_BYTES)


def _tile(n, pref):
    if n <= pref:
        return n
    t = (pref // V7X_LANES) * V7X_LANES
    while t >= V7X_LANES:
        if n % t == 0:
            return t
        t -= V7X_LANES
    return n


def _dot(a, b):
    return jnp.dot(a, b, preferred_element_type=F32)


def _dot_nt(a, b):
    return lax.dot_general(a, b, (((1,), (1,)), ((), ())), preferred_element_type=F32)


def _dot_tn(a, b):
    return lax.dot_general(a, b, (((0,), (0,)), ((), ())), preferred_element_type=F32)


def _silu(x):
    return x / (1.0 + jnp.exp(-x))


def _split_bf16(x):
    hi = x.astype(BF16)
    lo = (x - hi.astype(F32)).astype(BF16)
    return hi, lo


def _cond_kernel(c_ref, w_ref, b_ref, tab_ref, o_ref):
    s = _silu(c_ref[...]).astype(BF16)
    acc = _dot(s, w_ref[...].astype(BF16))
    o_ref[...] = acc[0:1, :] + b_ref[...] + tab_ref[...]


def _cond(c, w_ada, b_ada, ada_table):
    d = c.shape[1]
    depth = ada_table.shape[0]
    n = w_ada.shape[1]
    tn = _tile(n, 512)
    c8 = jnp.broadcast_to(c, (V7X_SUBLANES, d))
    return pl.pallas_call(
        _cond_kernel,
        grid=(n // tn,),
        in_specs=[pl.BlockSpec((V7X_SUBLANES, d), lambda j: (0, 0)),
                  pl.BlockSpec((d, tn), lambda j: (0, j)),
                  pl.BlockSpec((1, tn), lambda j: (0, j)),
                  pl.BlockSpec((depth, tn), lambda j: (0, j))],
        out_specs=pl.BlockSpec((depth, tn), lambda j: (0, j)),
        out_shape=jax.ShapeDtypeStruct((depth, n), F32),
        compiler_params=_params("arbitrary"),
        name="cond",
    )(c8, w_ada, b_ada.reshape(1, n), ada_table.reshape(depth, n))


def _norm_affine_kernel(x_ref, *refs, plus_one):
    n = len(plus_one)
    x = x_ref[...]
    ms = jnp.mean(x * x, axis=-1, keepdims=True)
    xn = x * lax.rsqrt(ms + EPS)
    for t in range(n):
        mul_ref, add_ref, o_ref = refs[2 * t], refs[2 * t + 1], refs[2 * n + t]
        mul = mul_ref[...]
        if plus_one[t]:
            mul = 1.0 + mul
        o_ref[...] = (xn * mul + add_ref[...]).astype(o_ref.dtype)


def _norm_affine(x, affines):
    s, d = x.shape
    tm = _tile(s, 512 if len(affines) == 1 else 256)
    vec = pl.BlockSpec((1, d), lambda i: (0, 0))
    row = pl.BlockSpec((tm, d), lambda i: (i, 0))
    operands = [v.reshape(1, d) for mul, add, _ in affines for v in (mul, add)]
    return pl.pallas_call(
        functools.partial(_norm_affine_kernel, plus_one=tuple(p for _, _, p in affines)),
        grid=(s // tm,),
        in_specs=[row] + [vec] * len(operands),
        out_specs=[row] * len(affines),
        out_shape=[jax.ShapeDtypeStruct((s, d), BF16)] * len(affines),
        compiler_params=_params("arbitrary"),
        name="norm_affine",
    )(x, *operands)


def _mm_plain_kernel(a_ref, w_ref, o_ref, *, valid_cols=None):
    w = w_ref[...]
    if valid_cols is not None:
        col = lax.broadcasted_iota(jnp.int32, w.shape, 1)
        w = jnp.where(col < valid_cols, w, jnp.zeros_like(w))
    o_ref[...] = _dot(a_ref[...], w).astype(o_ref.dtype)


def _mm_headnorm_kernel(a_ref, w_ref, g_ref, o_ref, *, post_scale):
    acc = _dot(a_ref[...], w_ref[...])
    g = g_ref[...] * post_scale
    for c in range(acc.shape[1] // SB_HEAD_DIM):
        sl = slice(c * SB_HEAD_DIM, (c + 1) * SB_HEAD_DIM)
        blk = acc[:, sl]
        ms = jnp.mean(blk * blk, axis=-1, keepdims=True)
        o_ref[:, sl] = (blk * lax.rsqrt(ms + EPS) * g).astype(o_ref.dtype)


def _mm_residual_kernel(a_ref, w_ref, x_ref, gate_ref, o_ref):
    o_ref[...] = x_ref[...] + gate_ref[...] * _dot(a_ref[...], w_ref[...])


def _mm_tiles(m, n, k):
    tm = _tile(m, 1024)
    tn = _tile(n, 1024 if k <= 4096 else 512)
    if k > 4096:
        tm = _tile(m, 512)
    return tm, tn


def _weight_spec(w, layer, col0, tn):
    k = w.shape[-2]
    assert col0 % tn == 0
    c0 = col0 // tn
    if w.ndim == 2:
        return pl.BlockSpec((k, tn), lambda i, j: (0, c0 + j))
    return pl.BlockSpec((None, k, tn), lambda i, j: (layer, 0, c0 + j))


def _mm_plain(a, w, out_dtype, *, layer=0, col0=0, n=None, valid_cols=None):
    m, k = a.shape
    n = w.shape[-1] if n is None else n
    tm, tn = _mm_tiles(m, n, k)
    return pl.pallas_call(
        functools.partial(_mm_plain_kernel, valid_cols=valid_cols),
        grid=(m // tm, n // tn),
        in_specs=[pl.BlockSpec((tm, k), lambda i, j: (i, 0)),
                  _weight_spec(w, layer, col0, tn)],
        out_specs=pl.BlockSpec((tm, tn), lambda i, j: (i, j)),
        out_shape=jax.ShapeDtypeStruct((m, n), out_dtype),
        compiler_params=_params("arbitrary", "arbitrary"),
        name="mm_plain",
    )(a, w)


def _mm_headnorm(a, w, g, post_scale, *, col0=0, n=None):
    m, k = a.shape
    n = w.shape[-1] if n is None else n
    tm, tn = _mm_tiles(m, n, k)
    return pl.pallas_call(
        functools.partial(_mm_headnorm_kernel, post_scale=post_scale),
        grid=(m // tm, n // tn),
        in_specs=[pl.BlockSpec((tm, k), lambda i, j: (i, 0)),
                  _weight_spec(w, 0, col0, tn),
                  pl.BlockSpec((1, SB_HEAD_DIM), lambda i, j: (0, 0))],
        out_specs=pl.BlockSpec((tm, tn), lambda i, j: (i, j)),
        out_shape=jax.ShapeDtypeStruct((m, n), BF16),
        compiler_params=_params("arbitrary", "arbitrary"),
        name="mm_headnorm",
    )(a, w, g.reshape(1, SB_HEAD_DIM))


def _mm_residual(a, w, x, gate):
    m, k = a.shape
    n = w.shape[-1]
    tm, tn = _mm_tiles(m, n, k)
    return pl.pallas_call(
        _mm_residual_kernel,
        grid=(m // tm, n // tn),
        in_specs=[pl.BlockSpec((tm, k), lambda i, j: (i, 0)),
                  _weight_spec(w, 0, 0, tn),
                  pl.BlockSpec((tm, tn), lambda i, j: (i, j)),
                  pl.BlockSpec((1, tn), lambda i, j: (0, j))],
        out_specs=pl.BlockSpec((tm, tn), lambda i, j: (i, j)),
        out_shape=jax.ShapeDtypeStruct((m, n), F32),
        compiler_params=_params("arbitrary", "arbitrary"),
        name="mm_residual",
    )(a, w, x, gate.reshape(1, n))


FFN_CHUNK_ROWS = 128


def _ffn_in_kernel(h_ref, wa_ref, wb_ref, cwa_ref, cwb_ref, cba_ref, cbb_ref, *refs, n_side):
    side_f32_refs = refs[:n_side]
    o_ref = refs[n_side]
    side_bf_refs = refs[n_side + 1:2 * n_side + 1]
    wa_bf_ref, wb_bf_ref, carry_a_ref, carry_b_ref, ua_ref, ub_ref = refs[2 * n_side + 1:]
    tm = h_ref.shape[0]
    cm = min(FFN_CHUNK_ROWS, tm)

    for src_ref, dst_ref in zip(side_f32_refs, side_bf_refs):
        dst_ref[...] = src_ref[...].astype(BF16)

    @pl.when(pl.program_id(1) == 0)
    def _():
        wa_bf_ref[...] = wa_ref[...].astype(BF16)
        wb_bf_ref[...] = wb_ref[...].astype(BF16)
        carry_a_ref[...] = jnp.zeros_like(carry_a_ref)
        carry_b_ref[...] = jnp.zeros_like(carry_b_ref)

    row = lax.broadcasted_iota(jnp.int32, carry_a_ref.shape, 0)

    def conv(u, prev, cw_ref, cb_ref):
        w0, w1, w2 = cw_ref[0:1, :], cw_ref[1:2, :], cw_ref[2:3, :]
        bias = cb_ref[...]
        u1 = pltpu.roll(u, 1, axis=0)
        u2 = pltpu.roll(u, 2, axis=0)
        head1 = jnp.where(row < 1, pltpu.roll(prev, 1, axis=0), u1[0:V7X_SUBLANES])
        head2 = jnp.where(row < 2, pltpu.roll(prev, 2, axis=0), u2[0:V7X_SUBLANES])
        head = bias + head2 * w0 + head1 * w1 + u[0:V7X_SUBLANES] * w2
        tail = (bias + u2[V7X_SUBLANES:] * w0 + u1[V7X_SUBLANES:] * w1
                + u[V7X_SUBLANES:] * w2)
        return head, tail

    prev_a = carry_a_ref[...]
    prev_b = carry_b_ref[...]
    for c in range(tm // cm):
        r0 = c * cm
        h = h_ref[r0:r0 + cm, :]
        slot = lax.rem(pl.program_id(1) + c, 2)
        ua_ref[slot] = _dot(h, wa_bf_ref[...])
        ub_ref[slot] = _dot(h, wb_bf_ref[...])
        ua = ua_ref[slot]
        ub = ub_ref[slot]
        a_head, a_tail = conv(ua, prev_a, cwa_ref, cba_ref)
        b_head, b_tail = conv(ub, prev_b, cwb_ref, cbb_ref)
        o_ref[r0:r0 + V7X_SUBLANES, :] = (_silu(a_head) * b_head).astype(o_ref.dtype)
        o_ref[r0 + V7X_SUBLANES:r0 + cm, :] = (_silu(a_tail) * b_tail).astype(o_ref.dtype)
        prev_a = ua[cm - V7X_SUBLANES:cm]
        prev_b = ub[cm - V7X_SUBLANES:cm]
    carry_a_ref[...] = prev_a
    carry_b_ref[...] = prev_b


def _slab_rows(rows, steps):
    packed = 2 * V7X_SUBLANES
    return min(m for m in range(packed, rows + 1, packed) if rows % m == 0 and m * steps >= rows)


def _ffn_in(h, w_in, conv_w, conv_b, layer, to_round):
    s, d = h.shape
    f = w_in.shape[-1] // 2
    tm = _tile(s, 1024)
    tn = _tile(f, 256)
    nj = f // tn
    ni = s // tm
    cm = min(FFN_CHUNK_ROWS, tm)
    conv_b = conv_b.reshape(conv_b.shape[0], 1, 2 * f)

    side_in_specs, side_out_specs, side_out_shapes = [], [], []
    for w, w_layer in to_round:
        rows, cols = w.shape[-2:]
        slab = _slab_rows(rows, nj * ni)
        last = rows // slab - 1

        def slab_index(j, i, last=last):
            return jnp.minimum(j * ni + i, last)

        side_in_specs.append(pl.BlockSpec(
            (None, slab, cols), lambda j, i, w_layer=w_layer, f=slab_index: (w_layer, f(j, i), 0)))
        side_out_specs.append(pl.BlockSpec((slab, cols), lambda j, i, f=slab_index: (f(j, i), 0)))
        side_out_shapes.append(jax.ShapeDtypeStruct((rows, cols), BF16))

    outs = pl.pallas_call(
        functools.partial(_ffn_in_kernel, n_side=len(to_round)),
        grid=(nj, ni),
        in_specs=[pl.BlockSpec((tm, d), lambda j, i: (i, 0)),
                  pl.BlockSpec((None, d, tn), lambda j, i: (layer, 0, j)),
                  pl.BlockSpec((None, d, tn), lambda j, i: (layer, 0, j + nj)),
                  pl.BlockSpec((None, CONV_WIDTH, tn), lambda j, i: (layer, 0, j)),
                  pl.BlockSpec((None, CONV_WIDTH, tn), lambda j, i: (layer, 0, j + nj)),
                  pl.BlockSpec((None, 1, tn), lambda j, i: (layer, 0, j)),
                  pl.BlockSpec((None, 1, tn), lambda j, i: (layer, 0, j + nj))] + side_in_specs,
        out_specs=[pl.BlockSpec((tm, tn), lambda j, i: (i, j))] + side_out_specs,
        out_shape=[jax.ShapeDtypeStruct((s, f), BF16)] + side_out_shapes,
        scratch_shapes=[pltpu.VMEM((d, tn), BF16),
                        pltpu.VMEM((d, tn), BF16),
                        pltpu.VMEM((V7X_SUBLANES, tn), F32),
                        pltpu.VMEM((V7X_SUBLANES, tn), F32),
                        pltpu.VMEM((2, cm, tn), F32),
                        pltpu.VMEM((2, cm, tn), F32)],
        compiler_params=_params("arbitrary", "arbitrary"),
        name="ffn_in",
    )(h, w_in, w_in, conv_w, conv_w, conv_b, conv_b, *(w for w, _ in to_round))
    return outs[0], list(outs[1:])


def _gla_kernel(qk_ref, v_ref, r_ref, gl_ref, wgu_ref, bg_ref, ng_ref, tril_ref,
                o_ref, state_ref, *, pairs, dk, dv):
    kw = GLA_HEADS * dk
    pair_rows = 2 * GLA_CHUNK

    @pl.when(pl.program_id(0) == 0)
    def _():
        state_ref[...] = jnp.zeros_like(state_ref)

    tril = tril_ref[...]
    own_causal = tril > 0
    r_id = lax.broadcasted_iota(jnp.int32, (pair_rows, pair_rows), 0)
    c_id = lax.broadcasted_iota(jnp.int32, (pair_rows, pair_rows), 1)
    b_sees_a = jnp.logical_and(r_id >= GLA_CHUNK, c_id < GLA_CHUNK)
    in_b = lax.broadcasted_iota(jnp.int32, (pair_rows, 1), 0) >= GLA_CHUNK
    q_scale = dk ** -0.5

    def pair(p, carry):
        rows = pl.ds(pl.multiple_of(p * pair_rows, pair_rows), pair_rows)
        y = _dot(gl_ref[rows, :].astype(BF16), wgu_ref[...]) + bg_ref[...]
        log_alpha = (jnp.minimum(y, 0.0) - jnp.log(1.0 + jnp.exp(-jnp.abs(y)))) * (1.0 / GLA_TAU)
        hi, lo = _split_bf16(log_alpha)
        b_cum = _dot(tril, hi) + _dot(tril, lo)
        b_last_a = b_cum[GLA_CHUNK - 1:GLA_CHUNK, :]
        b_last_b = b_cum[pair_rows - 1:pair_rows, :]
        d_a = jnp.exp(b_last_a)
        d_b = jnp.exp(b_last_b)
        e_pos = jnp.exp(b_cum)
        e_neg = jnp.exp(-b_cum)
        e_end = jnp.exp(jnp.where(in_b, b_last_b, b_last_a) - b_cum)
        e_pos_s = e_pos * jnp.where(in_b, d_a, 1.0)
        e_end_s = e_end * jnp.where(in_b, 1.0, d_b)
        decay_rows = jnp.broadcast_to(d_a * d_b, (V7X_LANES, kw))
        for h in range(GLA_HEADS):
            ksl = slice(h * dk, (h + 1) * dk)
            vsl = slice(h * dv, (h + 1) * dv)
            q = qk_ref[rows, ksl] * q_scale
            k = qk_ref[rows, kw + h * dk:kw + (h + 1) * dk]
            q_dec = (q * e_pos[:, ksl]).astype(BF16)
            k_dec = (k * e_neg[:, ksl]).astype(BF16)
            k_end = (k * e_end[:, ksl]).astype(BF16)
            q_dec_s = (q * e_pos_s[:, ksl]).astype(BF16)
            k_end_s = (k * e_end_s[:, ksl]).astype(BF16)
            v = v_ref[rows, vsl]
            scores = jnp.where(own_causal, _dot_nt(q_dec, k_dec),
                               jnp.where(b_sees_a, _dot_nt(q_dec, k_end), 0.0)).astype(BF16)
            state = state_ref[h]
            o = _dot(scores, v) + _dot(q_dec_s, state.astype(BF16))
            decay = jnp.concatenate(
                [decay_rows[:, h * dk + b * V7X_LANES:h * dk + (b + 1) * V7X_LANES].T
                 for b in range(dk // V7X_LANES)], axis=0)
            decay = jnp.concatenate([decay] * (dv // V7X_LANES), axis=1)
            state_ref[h] = state * decay + _dot_tn(k_end_s, v)
            ms = jnp.mean(o * o, axis=-1, keepdims=True)
            o = o * lax.rsqrt(ms + EPS) * ng_ref[...]
            o_ref[rows, vsl] = (_silu(r_ref[rows, vsl]) * o).astype(o_ref.dtype)
        return carry

    lax.fori_loop(0, pairs, pair, 0)


def _gla(qk, v, r, g_low, w_gate_up, b_gate, norm_g):
    s = qk.shape[0]
    kw = w_gate_up.shape[1]
    vw = v.shape[1]
    dk, dv = kw // GLA_HEADS, vw // GLA_HEADS
    rows = _tile(s, 256)
    pair_rows = 2 * GLA_CHUNK
    pairs = rows // pair_rows
    t = jnp.arange(pair_rows)
    tril = jnp.logical_and(t[:, None] >= t[None, :],
                           t[:, None] // GLA_CHUNK == t[None, :] // GLA_CHUNK).astype(BF16)
    return pl.pallas_call(
        functools.partial(_gla_kernel, pairs=pairs, dk=dk, dv=dv),
        grid=(s // rows,),
        in_specs=[pl.BlockSpec((rows, 2 * kw), lambda i: (i, 0)),
                  pl.BlockSpec((rows, vw), lambda i: (i, 0)),
                  pl.BlockSpec((rows, vw), lambda i: (i, 0)),
                  pl.BlockSpec((rows, V7X_LANES), lambda i: (i, 0)),
                  pl.BlockSpec((V7X_LANES, kw), lambda i: (0, 0)),
                  pl.BlockSpec((1, kw), lambda i: (0, 0)),
                  pl.BlockSpec((1, dv), lambda i: (0, 0)),
                  pl.BlockSpec((pair_rows, pair_rows), lambda i: (0, 0))],
        out_specs=pl.BlockSpec((rows, vw), lambda i: (i, 0)),
        out_shape=jax.ShapeDtypeStruct((s, vw), BF16),
        scratch_shapes=[pltpu.VMEM((GLA_HEADS, dk, dv), F32)],
        compiler_params=_params("arbitrary"),
        name="gla",
    )(qk, v, r, g_low, w_gate_up, b_gate.reshape(1, kw), norm_g.reshape(1, dv), tril)


def _sb_kernel(q_ref, k_ref, v_ref, t_ref, o_ref, acc_ref, run_ref, *, tq, tk, heads):
    qi = pl.program_id(1)
    tmat = t_ref[...]
    row_groups = tq // tk
    base = qi * row_groups
    below_diag = (lax.broadcasted_iota(jnp.int32, (tk, tk), 1)
                  < lax.broadcasted_iota(jnp.int32, (tk, tk), 0))

    chains = heads * row_groups
    all_chains = range(chains)
    group = [c % row_groups for c in all_chains]
    rows = [slice(g * tk, (g + 1) * tk) for g in group]
    lanes = [slice((c // row_groups) * SB_HEAD_DIM, (c // row_groups + 1) * SB_HEAD_DIM)
             for c in all_chains]

    def sweep(blocks, diagonal, live):
        starts = [pl.multiple_of(blocks[g] * tk, tk) for g in group]
        if live is not None:
            live = [live[g] for g in group]
        zs = [_dot_nt(q_ref[rows[c], lanes[c]], k_ref[pl.ds(starts[c], tk), lanes[c]])
              for c in all_chains]
        log_betas, halves = [], []
        for z in zs:
            softplus = jnp.maximum(z, 0.0) + jnp.log(1.0 + jnp.exp(-jnp.abs(z)))
            log_betas.append(z - softplus)
            neg_log_rest = softplus
            if diagonal:
                neg_log_rest = jnp.where(below_diag, neg_log_rest, 0.0)
            halves.append(jnp.concatenate(_split_bf16(neg_log_rest), axis=1))
        cms = [_dot(hl, tmat) for hl in halves]
        atts, runs = [], []
        for c in all_chains:
            log_att = log_betas[c] + cms[c][:, :tk]
            if diagonal:
                atts.append(jnp.where(below_diag, jnp.exp(log_att), 0.0))
                runs.append(cms[c][:, tk:])
            else:
                run = jnp.where(live[c], run_ref[rows[c], lanes[c]], -1e30)
                atts.append(jnp.exp(log_att + run))
                runs.append(run + cms[c][:, tk:])
        pvs = [_dot(atts[c].astype(BF16), v_ref[pl.ds(starts[c], tk), lanes[c]])
               for c in all_chains]
        if not diagonal:
            pvs = [acc_ref[rows[c], lanes[c]] + pvs[c] for c in all_chains]
        for c in all_chains:
            acc_ref[rows[c], lanes[c]] = pvs[c]
            run_ref[rows[c], lanes[c]] = runs[c]
        if diagonal:
            return jnp.float32(0.0)
        return jnp.max(functools.reduce(jnp.maximum, runs))

    top = sweep([base + g for g in range(row_groups)], True, None)

    def cond(st):
        n, top = st
        return jnp.logical_and(n <= base + row_groups - 1, top > F32_EXP_UNDERFLOW)

    def body(st):
        n, _ = st
        blocks = [base + g - n for g in range(row_groups)]
        top = sweep([jnp.maximum(j, 0) for j in blocks], False, [j >= 0 for j in blocks])
        return n + 1, top

    lax.while_loop(cond, body, (jnp.int32(1), top))
    o_ref[...] = acc_ref[...].astype(o_ref.dtype)


def _sb_attention(q, k, v):
    s, d = q.shape
    dh = SB_HEAD_DIM
    tk = V7X_LANES
    tq = _tile(s, 1024)
    r = lax.broadcasted_iota(jnp.int32, (tk, 2 * tk), 0)
    c = lax.broadcasted_iota(jnp.int32, (tk, 2 * tk), 1)
    half = jnp.where(jnp.logical_or(c >= tk, r > c), -1.0, 0.0).astype(BF16)
    tmat = jnp.concatenate([half, half], axis=0)
    heads = SB_HEADS_PER_STEP
    wh = heads * dh
    return pl.pallas_call(
        functools.partial(_sb_kernel, tq=tq, tk=tk, heads=heads),
        grid=(d // wh, s // tq),
        in_specs=[pl.BlockSpec((tq, wh), lambda h, i: (i, h)),
                  pl.BlockSpec((s, wh), lambda h, i: (0, h)),
                  pl.BlockSpec((s, wh), lambda h, i: (0, h)),
                  pl.BlockSpec((2 * tk, 2 * tk), lambda h, i: (0, 0))],
        out_specs=pl.BlockSpec((tq, wh), lambda h, i: (i, h)),
        out_shape=jax.ShapeDtypeStruct((s, d), BF16),
        scratch_shapes=[pltpu.VMEM((tq, wh), F32), pltpu.VMEM((tq, wh), F32)],
        compiler_params=_params("arbitrary", "arbitrary"),
        name="sb_attention",
    )(q, k, v, tmat)


def kernel(x, c, w_ada, b_ada, ada_table, gla_w_in, gla_w_gate_up, gla_b_gate, gla_norm_g, gla_w_out, kv_norm_g, w_kv, k_norm_g, sb_w_q, sb_q_norm_g, sb_w_out, ffn_w_in, ffn_conv_w, ffn_conv_b, ffn_w_out):
    batch, s, d = x.shape
    assert batch == 1, "adaLN modulation rows are built for a single sequence"
    depth = ada_table.shape[0]
    n_a = gla_w_in.shape[0]
    kw = gla_w_gate_up.shape[2]
    rank = gla_w_gate_up.shape[1]
    x = x.reshape(s, d)

    w_up_bf = jnp.pad(gla_w_gate_up, ((0, 0), (0, V7X_LANES - rank), (0, 0))).astype(BF16)
    w_kv3 = w_kv.reshape(1, d, 2 * d)

    def mixer_weights(l):
        if l >= depth:
            return []
        if l < n_a:
            return [(gla_w_out, l)]
        own = [(sb_w_q, l - n_a), (sb_w_out, l - n_a)]
        return ([(w_kv3, 0)] if l == n_a else []) + own

    gla_w_in_bf = gla_w_in.astype(BF16)
    mixer_bf = [w[i].astype(BF16) for w, i in mixer_weights(0)]

    mods = _cond(c, w_ada, b_ada, ada_table)
    k_sh = v_sh = None
    for l in range(depth):
        mod = mods[l]
        shift, scale, gate = (mod[i * d:(i + 1) * d] for i in (0, 1, 2))
        affines = [(scale, shift, True)]
        if l == n_a:
            affines.append((kv_norm_g, jnp.zeros_like(kv_norm_g), False))
        h, *kv_src = _norm_affine(x, affines)
        if l < n_a:
            (w_out_bf,) = mixer_bf
            qk = _mm_plain(h, gla_w_in_bf, F32, layer=l, col0=0, n=2 * kw)
            v = _mm_plain(h, gla_w_in_bf, BF16, layer=l, col0=2 * kw, n=d)
            r = _mm_plain(h, gla_w_in_bf, F32, layer=l, col0=2 * kw + d, n=d)
            g_low = _mm_plain(h, gla_w_in_bf, F32, layer=l, col0=2 * kw + 2 * d,
                              n=V7X_LANES, valid_cols=rank)
            o = _gla(qk, v, r, g_low, w_up_bf[l], gla_b_gate[l], gla_norm_g[l])
        else:
            if l == n_a:
                w_kv_bf = mixer_bf.pop(0)
                (src,) = kv_src
                k_sh = _mm_headnorm(src, w_kv_bf, k_norm_g, 1.0, col0=0, n=d)
                v_sh = _mm_plain(src, w_kv_bf, BF16, col0=d, n=d)
            w_q_bf, w_out_bf = mixer_bf
            q = _mm_headnorm(h, w_q_bf, sb_q_norm_g[l - n_a], SB_HEAD_DIM ** -0.5)
            o = _sb_attention(q, k_sh, v_sh)
        x = _mm_residual(o, w_out_bf, x, gate)

        shift, scale, gate = (mod[i * d:(i + 1) * d] for i in (3, 4, 5))
        (h,) = _norm_affine(x, [(scale, shift, True)])
        g, rounded = _ffn_in(h, ffn_w_in, ffn_conv_w, ffn_conv_b, l,
                             [(ffn_w_out, l)] + mixer_weights(l + 1))
        x = _mm_residual(g, rounded[0], x, gate)
        mixer_bf = rounded[1:]
    return x.reshape(batch, s, d)
```

```python
import functools

import jax
import jax.numpy as jnp
from jax import lax
from jax.experimental import pallas as pl
from jax.experimental.pallas import tpu as pltpu

F32 = jnp.float32
BF16 = jnp.bfloat16

EPS = 1e-6
CONV_WIDTH = 3
GLA_CHUNK = 64
GLA_HEADS = 4
GLA_TAU = 16.0
SB_HEAD_DIM = 128
SB_HEADS_PER_STEP = 2

V7X_LANES = 128
V7X_SUBLANES = 8
V7X_VMEM_LIMIT_BYTES = 56 * 1024 * 1024

F32_EXP_UNDERFLOW = -88.0


def _params(*semantics):
    return pltpu.CompilerParams(dimension_semantics=semantics,
                                vmem_limit_bytes=V7X_VMEM_LIMIT_BYTES)


def _tile(n, pref):
    if n <= pref:
        return n
    t = (pref // V7X_LANES) * V7X_LANES
    while t >= V7X_LANES:
        if n % t == 0:
            return t
        t -= V7X_LANES
    return n


def _dot(a, b):
    return jnp.dot(a, b, preferred_element_type=F32)


def _dot_nt(a, b):
    return lax.dot_general(a, b, (((1,), (1,)), ((), ())), preferred_element_type=F32)


def _dot_tn(a, b):
    return lax.dot_general(a, b, (((0,), (0,)), ((), ())), preferred_element_type=F32)


def _silu(x):
    return x / (1.0 + jnp.exp(-x))


def _split_bf16(x):
    hi = x.astype(BF16)
    lo = (x - hi.astype(F32)).astype(BF16)
    return hi, lo


def _cond_kernel(c_ref, w_ref, b_ref, tab_ref, o_ref):
    s = _silu(c_ref[...]).astype(BF16)
    acc = _dot(s, w_ref[...].astype(BF16))
    o_ref[...] = acc[0:1, :] + b_ref[...] + tab_ref[...]


def _cond(c, w_ada, b_ada, ada_table):
    d = c.shape[1]
    depth = ada_table.shape[0]
    n = w_ada.shape[1]
    tn = _tile(n, 512)
    c8 = jnp.broadcast_to(c, (V7X_SUBLANES, d))
    return pl.pallas_call(
        _cond_kernel,
        grid=(n // tn,),
        in_specs=[pl.BlockSpec((V7X_SUBLANES, d), lambda j: (0, 0)),
                  pl.BlockSpec((d, tn), lambda j: (0, j)),
                  pl.BlockSpec((1, tn), lambda j: (0, j)),
                  pl.BlockSpec((depth, tn), lambda j: (0, j))],
        out_specs=pl.BlockSpec((depth, tn), lambda j: (0, j)),
        out_shape=jax.ShapeDtypeStruct((depth, n), F32),
        compiler_params=_params("arbitrary"),
        name="cond",
    )(c8, w_ada, b_ada.reshape(1, n), ada_table.reshape(depth, n))


def _norm_affine_kernel(x_ref, *refs, plus_one):
    n = len(plus_one)
    x = x_ref[...]
    ms = jnp.mean(x * x, axis=-1, keepdims=True)
    xn = x * lax.rsqrt(ms + EPS)
    for t in range(n):
        mul_ref, add_ref, o_ref = refs[2 * t], refs[2 * t + 1], refs[2 * n + t]
        mul = mul_ref[...]
        if plus_one[t]:
            mul = 1.0 + mul
        o_ref[...] = (xn * mul + add_ref[...]).astype(o_ref.dtype)


def _norm_affine(x, affines):
    s, d = x.shape
    tm = _tile(s, 512 if len(affines) == 1 else 256)
    vec = pl.BlockSpec((1, d), lambda i: (0, 0))
    row = pl.BlockSpec((tm, d), lambda i: (i, 0))
    operands = [v.reshape(1, d) for mul, add, _ in affines for v in (mul, add)]
    return pl.pallas_call(
        functools.partial(_norm_affine_kernel, plus_one=tuple(p for _, _, p in affines)),
        grid=(s // tm,),
        in_specs=[row] + [vec] * len(operands),
        out_specs=[row] * len(affines),
        out_shape=[jax.ShapeDtypeStruct((s, d), BF16)] * len(affines),
        compiler_params=_params("arbitrary"),
        name="norm_affine",
    )(x, *operands)


def _mm_plain_kernel(a_ref, w_ref, o_ref, *, valid_cols=None):
    w = w_ref[...]
    if valid_cols is not None:
        col = lax.broadcasted_iota(jnp.int32, w.shape, 1)
        w = jnp.where(col < valid_cols, w, jnp.zeros_like(w))
    o_ref[...] = _dot(a_ref[...], w).astype(o_ref.dtype)


def _mm_headnorm_kernel(a_ref, w_ref, g_ref, o_ref, *, post_scale):
    acc = _dot(a_ref[...], w_ref[...])
    g = g_ref[...] * post_scale
    for c in range(acc.shape[1] // SB_HEAD_DIM):
        sl = slice(c * SB_HEAD_DIM, (c + 1) * SB_HEAD_DIM)
        blk = acc[:, sl]
        ms = jnp.mean(blk * blk, axis=-1, keepdims=True)
        o_ref[:, sl] = (blk * lax.rsqrt(ms + EPS) * g).astype(o_ref.dtype)


def _mm_residual_kernel(a_ref, w_ref, x_ref, gate_ref, o_ref):
    o_ref[...] = x_ref[...] + gate_ref[...] * _dot(a_ref[...], w_ref[...])


def _mm_tiles(m, n, k):
    tm = _tile(m, 1024)
    tn = _tile(n, 1024 if k <= 4096 else 512)
    if k > 4096:
        tm = _tile(m, 512)
    return tm, tn


def _weight_spec(w, layer, col0, tn):
    k = w.shape[-2]
    assert col0 % tn == 0
    c0 = col0 // tn
    if w.ndim == 2:
        return pl.BlockSpec((k, tn), lambda i, j: (0, c0 + j))
    return pl.BlockSpec((None, k, tn), lambda i, j: (layer, 0, c0 + j))


def _mm_plain(a, w, out_dtype, *, layer=0, col0=0, n=None, valid_cols=None):
    m, k = a.shape
    n = w.shape[-1] if n is None else n
    tm, tn = _mm_tiles(m, n, k)
    return pl.pallas_call(
        functools.partial(_mm_plain_kernel, valid_cols=valid_cols),
        grid=(m // tm, n // tn),
        in_specs=[pl.BlockSpec((tm, k), lambda i, j: (i, 0)),
                  _weight_spec(w, layer, col0, tn)],
        out_specs=pl.BlockSpec((tm, tn), lambda i, j: (i, j)),
        out_shape=jax.ShapeDtypeStruct((m, n), out_dtype),
        compiler_params=_params("arbitrary", "arbitrary"),
        name="mm_plain",
    )(a, w)


def _mm_headnorm(a, w, g, post_scale, *, col0=0, n=None):
    m, k = a.shape
    n = w.shape[-1] if n is None else n
    tm, tn = _mm_tiles(m, n, k)
    return pl.pallas_call(
        functools.partial(_mm_headnorm_kernel, post_scale=post_scale),
        grid=(m // tm, n // tn),
        in_specs=[pl.BlockSpec((tm, k), lambda i, j: (i, 0)),
                  _weight_spec(w, 0, col0, tn),
                  pl.BlockSpec((1, SB_HEAD_DIM), lambda i, j: (0, 0))],
        out_specs=pl.BlockSpec((tm, tn), lambda i, j: (i, j)),
        out_shape=jax.ShapeDtypeStruct((m, n), BF16),
        compiler_params=_params("arbitrary", "arbitrary"),
        name="mm_headnorm",
    )(a, w, g.reshape(1, SB_HEAD_DIM))


def _mm_residual(a, w, x, gate):
    m, k = a.shape
    n = w.shape[-1]
    tm, tn = _mm_tiles(m, n, k)
    return pl.pallas_call(
        _mm_residual_kernel,
        grid=(m // tm, n // tn),
        in_specs=[pl.BlockSpec((tm, k), lambda i, j: (i, 0)),
                  _weight_spec(w, 0, 0, tn),
                  pl.BlockSpec((tm, tn), lambda i, j: (i, j)),
                  pl.BlockSpec((1, tn), lambda i, j: (0, j))],
        out_specs=pl.BlockSpec((tm, tn), lambda i, j: (i, j)),
        out_shape=jax.ShapeDtypeStruct((m, n), F32),
        compiler_params=_params("arbitrary", "arbitrary"),
        name="mm_residual",
    )(a, w, x, gate.reshape(1, n))


FFN_CHUNK_ROWS = 256


def _ffn_in_kernel(h_ref, wa_ref, wb_ref, cwa_ref, cwb_ref, cba_ref, cbb_ref, *refs, n_side):
    side_f32_refs = refs[:n_side]
    o_ref = refs[n_side]
    side_bf_refs = refs[n_side + 1:2 * n_side + 1]
    wa_bf_ref, wb_bf_ref, carry_a_ref, carry_b_ref, ua_ref, ub_ref = refs[2 * n_side + 1:]
    tm = h_ref.shape[0]
    cm = min(FFN_CHUNK_ROWS, tm)

    for src_ref, dst_ref in zip(side_f32_refs, side_bf_refs):
        dst_ref[...] = src_ref[...].astype(BF16)

    @pl.when(pl.program_id(1) == 0)
    def _():
        wa_bf_ref[...] = wa_ref[...].astype(BF16)
        wb_bf_ref[...] = wb_ref[...].astype(BF16)
        carry_a_ref[...] = jnp.zeros_like(carry_a_ref)
        carry_b_ref[...] = jnp.zeros_like(carry_b_ref)

    row = lax.broadcasted_iota(jnp.int32, carry_a_ref.shape, 0)

    def conv(u, prev, cw_ref, cb_ref):
        w0, w1, w2 = cw_ref[0:1, :], cw_ref[1:2, :], cw_ref[2:3, :]
        bias = cb_ref[...]
        u1 = pltpu.roll(u, 1, axis=0)
        u2 = pltpu.roll(u, 2, axis=0)
        head1 = jnp.where(row < 1, pltpu.roll(prev, 1, axis=0), u1[0:V7X_SUBLANES])
        head2 = jnp.where(row < 2, pltpu.roll(prev, 2, axis=0), u2[0:V7X_SUBLANES])
        head = bias + head2 * w0 + head1 * w1 + u[0:V7X_SUBLANES] * w2
        tail = (bias + u2[V7X_SUBLANES:] * w0 + u1[V7X_SUBLANES:] * w1
                + u[V7X_SUBLANES:] * w2)
        return head, tail

    prev_a = carry_a_ref[...]
    prev_b = carry_b_ref[...]
    for c in range(tm // cm):
        r0 = c * cm
        h = h_ref[r0:r0 + cm, :]
        slot = lax.rem(pl.program_id(1) + c, 2)
        ua_ref[slot] = _dot(h, wa_bf_ref[...])
        ub_ref[slot] = _dot(h, wb_bf_ref[...])
        ua = ua_ref[slot]
        ub = ub_ref[slot]
        a_head, a_tail = conv(ua, prev_a, cwa_ref, cba_ref)
        b_head, b_tail = conv(ub, prev_b, cwb_ref, cbb_ref)
        o_ref[r0:r0 + V7X_SUBLANES, :] = (_silu(a_head) * b_head).astype(o_ref.dtype)
        o_ref[r0 + V7X_SUBLANES:r0 + cm, :] = (_silu(a_tail) * b_tail).astype(o_ref.dtype)
        prev_a = ua[cm - V7X_SUBLANES:cm]
        prev_b = ub[cm - V7X_SUBLANES:cm]
    carry_a_ref[...] = prev_a
    carry_b_ref[...] = prev_b


def _slab_rows(rows, steps):
    packed = 2 * V7X_SUBLANES
    return min(m for m in range(packed, rows + 1, packed) if rows % m == 0 and m * steps >= rows)


def _ffn_in(h, w_in, conv_w, conv_b, layer, to_round):
    s, d = h.shape
    f = w_in.shape[-1] // 2
    tm = _tile(s, 1024)
    tn = _tile(f, 256)
    nj = f // tn
    ni = s // tm
    cm = min(FFN_CHUNK_ROWS, tm)
    conv_b = conv_b.reshape(conv_b.shape[0], 1, 2 * f)

    side_in_specs, side_out_specs, side_out_shapes = [], [], []
    for w, w_layer in to_round:
        rows, cols = w.shape[-2:]
        slab = _slab_rows(rows, nj * ni)
        last = rows // slab - 1

        def slab_index(j, i, last=last):
            return jnp.minimum(j * ni + i, last)

        side_in_specs.append(pl.BlockSpec(
            (None, slab, cols), lambda j, i, w_layer=w_layer, f=slab_index: (w_layer, f(j, i), 0)))
        side_out_specs.append(pl.BlockSpec((slab, cols), lambda j, i, f=slab_index: (f(j, i), 0)))
        side_out_shapes.append(jax.ShapeDtypeStruct((rows, cols), BF16))

    outs = pl.pallas_call(
        functools.partial(_ffn_in_kernel, n_side=len(to_round)),
        grid=(nj, ni),
        in_specs=[pl.BlockSpec((tm, d), lambda j, i: (i, 0)),
                  pl.BlockSpec((None, d, tn), lambda j, i: (layer, 0, j)),
                  pl.BlockSpec((None, d, tn), lambda j, i: (layer, 0, j + nj)),
                  pl.BlockSpec((None, CONV_WIDTH, tn), lambda j, i: (layer, 0, j)),
                  pl.BlockSpec((None, CONV_WIDTH, tn), lambda j, i: (layer, 0, j + nj)),
                  pl.BlockSpec((None, 1, tn), lambda j, i: (layer, 0, j)),
                  pl.BlockSpec((None, 1, tn), lambda j, i: (layer, 0, j + nj))] + side_in_specs,
        out_specs=[pl.BlockSpec((tm, tn), lambda j, i: (i, j))] + side_out_specs,
        out_shape=[jax.ShapeDtypeStruct((s, f), BF16)] + side_out_shapes,
        scratch_shapes=[pltpu.VMEM((d, tn), BF16),
                        pltpu.VMEM((d, tn), BF16),
                        pltpu.VMEM((V7X_SUBLANES, tn), F32),
                        pltpu.VMEM((V7X_SUBLANES, tn), F32),
                        pltpu.VMEM((2, cm, tn), F32),
                        pltpu.VMEM((2, cm, tn), F32)],
        compiler_params=_params("arbitrary", "arbitrary"),
        name="ffn_in",
    )(h, w_in, w_in, conv_w, conv_w, conv_b, conv_b, *(w for w, _ in to_round))
    return outs[0], list(outs[1:])


def _gla_kernel(qk_ref, v_ref, r_ref, gl_ref, wgu_ref, bg_ref, ng_ref, tril_ref,
                o_ref, state_ref, *, pairs, dk, dv):
    kw = GLA_HEADS * dk
    pair_rows = 2 * GLA_CHUNK

    @pl.when(pl.program_id(0) == 0)
    def _():
        state_ref[...] = jnp.zeros_like(state_ref)

    tril = tril_ref[...]
    own_causal = tril > 0
    r_id = lax.broadcasted_iota(jnp.int32, (pair_rows, pair_rows), 0)
    c_id = lax.broadcasted_iota(jnp.int32, (pair_rows, pair_rows), 1)
    b_sees_a = jnp.logical_and(r_id >= GLA_CHUNK, c_id < GLA_CHUNK)
    in_b = lax.broadcasted_iota(jnp.int32, (pair_rows, 1), 0) >= GLA_CHUNK
    q_scale = dk ** -0.5

    def pair(p, carry):
        rows = pl.ds(pl.multiple_of(p * pair_rows, pair_rows), pair_rows)
        y = _dot(gl_ref[rows, :].astype(BF16), wgu_ref[...]) + bg_ref[...]
        log_alpha = (jnp.minimum(y, 0.0) - jnp.log(1.0 + jnp.exp(-jnp.abs(y)))) * (1.0 / GLA_TAU)
        hi, lo = _split_bf16(log_alpha)
        b_cum = _dot(tril, hi) + _dot(tril, lo)
        b_last_a = b_cum[GLA_CHUNK - 1:GLA_CHUNK, :]
        b_last_b = b_cum[pair_rows - 1:pair_rows, :]
        d_a = jnp.exp(b_last_a)
        d_b = jnp.exp(b_last_b)
        e_pos = jnp.exp(b_cum)
        e_neg = jnp.exp(-b_cum)
        e_end = jnp.exp(jnp.where(in_b, b_last_b, b_last_a) - b_cum)
        e_pos_s = e_pos * jnp.where(in_b, d_a, 1.0)
        e_end_s = e_end * jnp.where(in_b, 1.0, d_b)
        decay_rows = jnp.broadcast_to(d_a * d_b, (V7X_LANES, kw))
        for h in range(GLA_HEADS):
            ksl = slice(h * dk, (h + 1) * dk)
            vsl = slice(h * dv, (h + 1) * dv)
            q = qk_ref[rows, ksl] * q_scale
            k = qk_ref[rows, kw + h * dk:kw + (h + 1) * dk]
            q_dec = (q * e_pos[:, ksl]).astype(BF16)
            k_dec = (k * e_neg[:, ksl]).astype(BF16)
            k_end = (k * e_end[:, ksl]).astype(BF16)
            q_dec_s = (q * e_pos_s[:, ksl]).astype(BF16)
            k_end_s = (k * e_end_s[:, ksl]).astype(BF16)
            v = v_ref[rows, vsl]
            scores = jnp.where(own_causal, _dot_nt(q_dec, k_dec),
                               jnp.where(b_sees_a, _dot_nt(q_dec, k_end), 0.0)).astype(BF16)
            state = state_ref[h]
            o = _dot(scores, v) + _dot(q_dec_s, state.astype(BF16))
            decay = jnp.concatenate(
                [decay_rows[:, h * dk + b * V7X_LANES:h * dk + (b + 1) * V7X_LANES].T
                 for b in range(dk // V7X_LANES)], axis=0)
            decay = jnp.concatenate([decay] * (dv // V7X_LANES), axis=1)
            state_ref[h] = state * decay + _dot_tn(k_end_s, v)
            ms = jnp.mean(o * o, axis=-1, keepdims=True)
            o = o * lax.rsqrt(ms + EPS) * ng_ref[...]
            o_ref[rows, vsl] = (_silu(r_ref[rows, vsl]) * o).astype(o_ref.dtype)
        return carry

    lax.fori_loop(0, pairs, pair, 0)


def _gla(qk, v, r, g_low, w_gate_up, b_gate, norm_g):
    s = qk.shape[0]
    kw = w_gate_up.shape[1]
    vw = v.shape[1]
    dk, dv = kw // GLA_HEADS, vw // GLA_HEADS
    rows = _tile(s, 256)
    pair_rows = 2 * GLA_CHUNK
    pairs = rows // pair_rows
    t = jnp.arange(pair_rows)
    tril = jnp.logical_and(t[:, None] >= t[None, :],
                           t[:, None] // GLA_CHUNK == t[None, :] // GLA_CHUNK).astype(BF16)
    return pl.pallas_call(
        functools.partial(_gla_kernel, pairs=pairs, dk=dk, dv=dv),
        grid=(s // rows,),
        in_specs=[pl.BlockSpec((rows, 2 * kw), lambda i: (i, 0)),
                  pl.BlockSpec((rows, vw), lambda i: (i, 0)),
                  pl.BlockSpec((rows, vw), lambda i: (i, 0)),
                  pl.BlockSpec((rows, V7X_LANES), lambda i: (i, 0)),
                  pl.BlockSpec((V7X_LANES, kw), lambda i: (0, 0)),
                  pl.BlockSpec((1, kw), lambda i: (0, 0)),
                  pl.BlockSpec((1, dv), lambda i: (0, 0)),
                  pl.BlockSpec((pair_rows, pair_rows), lambda i: (0, 0))],
        out_specs=pl.BlockSpec((rows, vw), lambda i: (i, 0)),
        out_shape=jax.ShapeDtypeStruct((s, vw), BF16),
        scratch_shapes=[pltpu.VMEM((GLA_HEADS, dk, dv), F32)],
        compiler_params=_params("arbitrary"),
        name="gla",
    )(qk, v, r, g_low, w_gate_up, b_gate.reshape(1, kw), norm_g.reshape(1, dv), tril)


def _sb_kernel(q_ref, k_ref, v_ref, t_ref, o_ref, acc_ref, run_ref, *, tq, tk, heads):
    qi = pl.program_id(1)
    tmat = t_ref[...]
    row_groups = tq // tk
    base = qi * row_groups
    below_diag = (lax.broadcasted_iota(jnp.int32, (tk, tk), 1)
                  < lax.broadcasted_iota(jnp.int32, (tk, tk), 0))

    chains = heads * row_groups
    all_chains = range(chains)
    group = [c % row_groups for c in all_chains]
    rows = [slice(g * tk, (g + 1) * tk) for g in group]
    lanes = [slice((c // row_groups) * SB_HEAD_DIM, (c // row_groups + 1) * SB_HEAD_DIM)
             for c in all_chains]

    def sweep(blocks, diagonal, live):
        starts = [pl.multiple_of(blocks[g] * tk, tk) for g in group]
        if live is not None:
            live = [live[g] for g in group]
        zs = [_dot_nt(q_ref[rows[c], lanes[c]], k_ref[pl.ds(starts[c], tk), lanes[c]])
              for c in all_chains]
        log_betas, halves = [], []
        for z in zs:
            softplus = jnp.maximum(z, 0.0) + jnp.log(1.0 + jnp.exp(-jnp.abs(z)))
            log_betas.append(z - softplus)
            neg_log_rest = softplus
            if diagonal:
                neg_log_rest = jnp.where(below_diag, neg_log_rest, 0.0)
            halves.append(jnp.concatenate(_split_bf16(neg_log_rest), axis=1))
        cms = [_dot(hl, tmat) for hl in halves]
        atts, runs = [], []
        for c in all_chains:
            log_att = log_betas[c] + cms[c][:, :tk]
            if diagonal:
                atts.append(jnp.where(below_diag, jnp.exp(log_att), 0.0))
                runs.append(cms[c][:, tk:])
            else:
                run = jnp.where(live[c], run_ref[rows[c], lanes[c]], -1e30)
                atts.append(jnp.exp(log_att + run))
                runs.append(run + cms[c][:, tk:])
        pvs = [_dot(atts[c].astype(BF16), v_ref[pl.ds(starts[c], tk), lanes[c]])
               for c in all_chains]
        if not diagonal:
            pvs = [acc_ref[rows[c], lanes[c]] + pvs[c] for c in all_chains]
        for c in all_chains:
            acc_ref[rows[c], lanes[c]] = pvs[c]
            run_ref[rows[c], lanes[c]] = runs[c]
        if diagonal:
            return jnp.float32(0.0)
        return jnp.max(functools.reduce(jnp.maximum, runs))

    top = sweep([base + g for g in range(row_groups)], True, None)

    def cond(st):
        n, top = st
        return jnp.logical_and(n <= base + row_groups - 1, top > F32_EXP_UNDERFLOW)

    def body(st):
        n, _ = st
        blocks = [base + g - n for g in range(row_groups)]
        top = sweep([jnp.maximum(j, 0) for j in blocks], False, [j >= 0 for j in blocks])
        return n + 1, top

    lax.while_loop(cond, body, (jnp.int32(1), top))
    o_ref[...] = acc_ref[...].astype(o_ref.dtype)


def _sb_attention(q, k, v):
    s, d = q.shape
    dh = SB_HEAD_DIM
    tk = V7X_LANES
    tq = _tile(s, 1024)
    r = lax.broadcasted_iota(jnp.int32, (tk, 2 * tk), 0)
    c = lax.broadcasted_iota(jnp.int32, (tk, 2 * tk), 1)
    half = jnp.where(jnp.logical_or(c >= tk, r > c), -1.0, 0.0).astype(BF16)
    tmat = jnp.concatenate([half, half], axis=0)
    heads = SB_HEADS_PER_STEP
    wh = heads * dh
    return pl.pallas_call(
        functools.partial(_sb_kernel, tq=tq, tk=tk, heads=heads),
        grid=(d // wh, s // tq),
        in_specs=[pl.BlockSpec((tq, wh), lambda h, i: (i, h)),
                  pl.BlockSpec((s, wh), lambda h, i: (0, h)),
                  pl.BlockSpec((s, wh), lambda h, i: (0, h)),
                  pl.BlockSpec((2 * tk, 2 * tk), lambda h, i: (0, 0))],
        out_specs=pl.BlockSpec((tq, wh), lambda h, i: (i, h)),
        out_shape=jax.ShapeDtypeStruct((s, d), BF16),
        scratch_shapes=[pltpu.VMEM((tq, wh), F32), pltpu.VMEM((tq, wh), F32)],
        compiler_params=_params("arbitrary", "arbitrary"),
        name="sb_attention",
    )(q, k, v, tmat)


def kernel(x, c, w_ada, b_ada, ada_table, gla_w_in, gla_w_gate_up, gla_b_gate, gla_norm_g, gla_w_out, kv_norm_g, w_kv, k_norm_g, sb_w_q, sb_q_norm_g, sb_w_out, ffn_w_in, ffn_conv_w, ffn_conv_b, ffn_w_out):
    batch, s, d = x.shape
    assert batch == 1, "adaLN modulation rows are built for a single sequence"
    depth = ada_table.shape[0]
    n_a = gla_w_in.shape[0]
    kw = gla_w_gate_up.shape[2]
    rank = gla_w_gate_up.shape[1]
    x = x.reshape(s, d)

    w_up_bf = jnp.pad(gla_w_gate_up, ((0, 0), (0, V7X_LANES - rank), (0, 0))).astype(BF16)
    w_kv3 = w_kv.reshape(1, d, 2 * d)

    def mixer_weights(l):
        if l >= depth:
            return []
        if l < n_a:
            return [(gla_w_out, l)]
        own = [(sb_w_q, l - n_a), (sb_w_out, l - n_a)]
        return ([(w_kv3, 0)] if l == n_a else []) + own

    gla_w_in_bf = gla_w_in.astype(BF16)
    mixer_bf = [w[i].astype(BF16) for w, i in mixer_weights(0)]

    mods = _cond(c, w_ada, b_ada, ada_table)
    k_sh = v_sh = None
    for l in range(depth):
        mod = mods[l]
        shift, scale, gate = (mod[i * d:(i + 1) * d] for i in (0, 1, 2))
        affines = [(scale, shift, True)]
        if l == n_a:
            affines.append((kv_norm_g, jnp.zeros_like(kv_norm_g), False))
        h, *kv_src = _norm_affine(x, affines)
        if l < n_a:
            (w_out_bf,) = mixer_bf
            qk = _mm_plain(h, gla_w_in_bf, F32, layer=l, col0=0, n=2 * kw)
            v = _mm_plain(h, gla_w_in_bf, BF16, layer=l, col0=2 * kw, n=d)
            r = _mm_plain(h, gla_w_in_bf, F32, layer=l, col0=2 * kw + d, n=d)
            g_low = _mm_plain(h, gla_w_in_bf, F32, layer=l, col0=2 * kw + 2 * d,
                              n=V7X_LANES, valid_cols=rank)
            o = _gla(qk, v, r, g_low, w_up_bf[l], gla_b_gate[l], gla_norm_g[l])
        else:
            if l == n_a:
                w_kv_bf = mixer_bf.pop(0)
                (src,) = kv_src
                k_sh = _mm_headnorm(src, w_kv_bf, k_norm_g, 1.0, col0=0, n=d)
                v_sh = _mm_plain(src, w_kv_bf, BF16, col0=d, n=d)
            w_q_bf, w_out_bf = mixer_bf
            q = _mm_headnorm(h, w_q_bf, sb_q_norm_g[l - n_a], SB_HEAD_DIM ** -0.5)
            o = _sb_attention(q, k_sh, v_sh)
        x = _mm_residual(o, w_out_bf, x, gate)

        shift, scale, gate = (mod[i * d:(i + 1) * d] for i in (3, 4, 5))
        (h,) = _norm_affine(x, [(scale, shift, True)])
        g, rounded = _ffn_in(h, ffn_w_in, ffn_conv_w, ffn_conv_b, l,
                             [(ffn_w_out, l)] + mixer_weights(l + 1))
        x = _mm_residual(g, rounded[0], x, gate)
        mixer_bf = rounded[1:]
    return x.reshape(batch, s, d)
```

```python
import functools

import jax
import jax.numpy as jnp
from jax import lax
from jax.experimental import pallas as pl
from jax.experimental.pallas import tpu as pltpu

F32 = jnp.float32
BF16 = jnp.bfloat16

EPS = 1e-6
CONV_WIDTH = 3
GLA_CHUNK = 64
GLA_HEADS = 4
GLA_TAU = 16.0
SB_HEAD_DIM = 128
SB_HEADS_PER_STEP = 2

V7X_LANES = 128
V7X_SUBLANES = 8
V7X_VMEM_LIMIT_BYTES = 56 * 1024 * 1024

F32_EXP_UNDERFLOW = -88.0


def _params(*semantics):
    return pltpu.CompilerParams(dimension_semantics=semantics,
                                vmem_limit_bytes=V7X_VMEM_LIMIT_BYTES)


def _tile(n, pref):
    if n <= pref:
        return n
    t = (pref // V7X_LANES) * V7X_LANES
    while t >= V7X_LANES:
        if n % t == 0:
            return t
        t -= V7X_LANES
    return n


def _dot(a, b):
    return jnp.dot(a, b, preferred_element_type=F32)


def _dot_nt(a, b):
    return lax.dot_general(a, b, (((1,), (1,)), ((), ())), preferred_element_type=F32)


def _dot_tn(a, b):
    return lax.dot_general(a, b, (((0,), (0,)), ((), ())), preferred_element_type=F32)


def _silu(x):
    return x / (1.0 + jnp.exp(-x))


def _split_bf16(x):
    hi = x.astype(BF16)
    lo = (x - hi.astype(F32)).astype(BF16)
    return hi, lo


def _cond_kernel(c_ref, w_ref, b_ref, tab_ref, o_ref):
    s = _silu(c_ref[...]).astype(BF16)
    acc = _dot(s, w_ref[...].astype(BF16))
    o_ref[...] = acc[0:1, :] + b_ref[...] + tab_ref[...]


def _cond(c, w_ada, b_ada, ada_table):
    d = c.shape[1]
    depth = ada_table.shape[0]
    n = w_ada.shape[1]
    tn = _tile(n, 512)
    c8 = jnp.broadcast_to(c, (V7X_SUBLANES, d))
    return pl.pallas_call(
        _cond_kernel,
        grid=(n // tn,),
        in_specs=[pl.BlockSpec((V7X_SUBLANES, d), lambda j: (0, 0)),
                  pl.BlockSpec((d, tn), lambda j: (0, j)),
                  pl.BlockSpec((1, tn), lambda j: (0, j)),
                  pl.BlockSpec((depth, tn), lambda j: (0, j))],
        out_specs=pl.BlockSpec((depth, tn), lambda j: (0, j)),
        out_shape=jax.ShapeDtypeStruct((depth, n), F32),
        compiler_params=_params("arbitrary"),
        name="cond",
    )(c8, w_ada, b_ada.reshape(1, n), ada_table.reshape(depth, n))


def _norm_affine_kernel(x_ref, *refs, plus_one):
    n = len(plus_one)
    x = x_ref[...]
    ms = jnp.mean(x * x, axis=-1, keepdims=True)
    xn = x * lax.rsqrt(ms + EPS)
    for t in range(n):
        mul_ref, add_ref, o_ref = refs[2 * t], refs[2 * t + 1], refs[2 * n + t]
        mul = mul_ref[...]
        if plus_one[t]:
            mul = 1.0 + mul
        o_ref[...] = (xn * mul + add_ref[...]).astype(o_ref.dtype)


def _norm_affine(x, affines):
    s, d = x.shape
    tm = _tile(s, 512 if len(affines) == 1 else 256)
    vec = pl.BlockSpec((1, d), lambda i: (0, 0))
    row = pl.BlockSpec((tm, d), lambda i: (i, 0))
    operands = [v.reshape(1, d) for mul, add, _ in affines for v in (mul, add)]
    return pl.pallas_call(
        functools.partial(_norm_affine_kernel, plus_one=tuple(p for _, _, p in affines)),
        grid=(s // tm,),
        in_specs=[row] + [vec] * len(operands),
        out_specs=[row] * len(affines),
        out_shape=[jax.ShapeDtypeStruct((s, d), BF16)] * len(affines),
        compiler_params=_params("arbitrary"),
        name="norm_affine",
    )(x, *operands)


def _mm_plain_kernel(a_ref, w_ref, o_ref, *, valid_cols=None):
    w = w_ref[...]
    if valid_cols is not None:
        col = lax.broadcasted_iota(jnp.int32, w.shape, 1)
        w = jnp.where(col < valid_cols, w, jnp.zeros_like(w))
    o_ref[...] = _dot(a_ref[...], w).astype(o_ref.dtype)


def _mm_headnorm_kernel(a_ref, w_ref, g_ref, o_ref, *, post_scale):
    acc = _dot(a_ref[...], w_ref[...])
    g = g_ref[...] * post_scale
    for c in range(acc.shape[1] // SB_HEAD_DIM):
        sl = slice(c * SB_HEAD_DIM, (c + 1) * SB_HEAD_DIM)
        blk = acc[:, sl]
        ms = jnp.mean(blk * blk, axis=-1, keepdims=True)
        o_ref[:, sl] = (blk * lax.rsqrt(ms + EPS) * g).astype(o_ref.dtype)


def _mm_residual_kernel(a_ref, w_ref, x_ref, gate_ref, o_ref):
    o_ref[...] = x_ref[...] + gate_ref[...] * _dot(a_ref[...], w_ref[...])


def _mm_tiles(m, n, k):
    tm = _tile(m, 1024)
    tn = _tile(n, 1024 if k <= 4096 else 512)
    if k > 4096:
        tm = _tile(m, 512)
    return tm, tn


def _weight_spec(w, layer, col0, tn):
    k = w.shape[-2]
    assert col0 % tn == 0
    c0 = col0 // tn
    if w.ndim == 2:
        return pl.BlockSpec((k, tn), lambda i, j: (0, c0 + j))
    return pl.BlockSpec((None, k, tn), lambda i, j: (layer, 0, c0 + j))


def _mm_plain(a, w, out_dtype, *, layer=0, col0=0, n=None, valid_cols=None):
    m, k = a.shape
    n = w.shape[-1] if n is None else n
    tm, tn = _mm_tiles(m, n, k)
    return pl.pallas_call(
        functools.partial(_mm_plain_kernel, valid_cols=valid_cols),
        grid=(m // tm, n // tn),
        in_specs=[pl.BlockSpec((tm, k), lambda i, j: (i, 0)),
                  _weight_spec(w, layer, col0, tn)],
        out_specs=pl.BlockSpec((tm, tn), lambda i, j: (i, j)),
        out_shape=jax.ShapeDtypeStruct((m, n), out_dtype),
        compiler_params=_params("arbitrary", "arbitrary"),
        name="mm_plain",
    )(a, w)


def _mm_headnorm(a, w, g, post_scale, *, col0=0, n=None):
    m, k = a.shape
    n = w.shape[-1] if n is None else n
    tm, tn = _mm_tiles(m, n, k)
    return pl.pallas_call(
        functools.partial(_mm_headnorm_kernel, post_scale=post_scale),
        grid=(m // tm, n // tn),
        in_specs=[pl.BlockSpec((tm, k), lambda i, j: (i, 0)),
                  _weight_spec(w, 0, col0, tn),
                  pl.BlockSpec((1, SB_HEAD_DIM), lambda i, j: (0, 0))],
        out_specs=pl.BlockSpec((tm, tn), lambda i, j: (i, j)),
        out_shape=jax.ShapeDtypeStruct((m, n), BF16),
        compiler_params=_params("arbitrary", "arbitrary"),
        name="mm_headnorm",
    )(a, w, g.reshape(1, SB_HEAD_DIM))


def _mm_residual(a, w, x, gate):
    m, k = a.shape
    n = w.shape[-1]
    tm, tn = _mm_tiles(m, n, k)
    return pl.pallas_call(
        _mm_residual_kernel,
        grid=(m // tm, n // tn),
        in_specs=[pl.BlockSpec((tm, k), lambda i, j: (i, 0)),
                  _weight_spec(w, 0, 0, tn),
                  pl.BlockSpec((tm, tn), lambda i, j: (i, j)),
                  pl.BlockSpec((1, tn), lambda i, j: (0, j))],
        out_specs=pl.BlockSpec((tm, tn), lambda i, j: (i, j)),
        out_shape=jax.ShapeDtypeStruct((m, n), F32),
        compiler_params=_params("arbitrary", "arbitrary"),
        name="mm_residual",
    )(a, w, x, gate.reshape(1, n))


FFN_CHUNK_ROWS = 128


def _ffn_in_kernel(h_ref, wa_ref, wb_ref, cwa_ref, cwb_ref, cba_ref, cbb_ref, *refs, n_side):
    side_f32_refs = refs[:n_side]
    o_ref = refs[n_side]
    side_bf_refs = refs[n_side + 1:2 * n_side + 1]
    wa_bf_ref, wb_bf_ref, carry_a_ref, carry_b_ref, ua_ref, ub_ref = refs[2 * n_side + 1:]
    tm = h_ref.shape[0]
    cm = min(FFN_CHUNK_ROWS, tm)

    for src_ref, dst_ref in zip(side_f32_refs, side_bf_refs):
        dst_ref[...] = src_ref[...].astype(BF16)

    @pl.when(pl.program_id(1) == 0)
    def _():
        wa_bf_ref[...] = wa_ref[...].astype(BF16)
        wb_bf_ref[...] = wb_ref[...].astype(BF16)
        carry_a_ref[...] = jnp.zeros_like(carry_a_ref)
        carry_b_ref[...] = jnp.zeros_like(carry_b_ref)

    row = lax.broadcasted_iota(jnp.int32, carry_a_ref.shape, 0)

    def conv(u, prev, cw_ref, cb_ref):
        w0, w1, w2 = cw_ref[0:1, :], cw_ref[1:2, :], cw_ref[2:3, :]
        bias = cb_ref[...]
        u1 = pltpu.roll(u, 1, axis=0)
        u2 = pltpu.roll(u, 2, axis=0)
        head1 = jnp.where(row < 1, pltpu.roll(prev, 1, axis=0), u1[0:V7X_SUBLANES])
        head2 = jnp.where(row < 2, pltpu.roll(prev, 2, axis=0), u2[0:V7X_SUBLANES])
        head = bias + head2 * w0 + head1 * w1 + u[0:V7X_SUBLANES] * w2
        tail = (bias + u2[V7X_SUBLANES:] * w0 + u1[V7X_SUBLANES:] * w1
                + u[V7X_SUBLANES:] * w2)
        return head, tail

    prev_a = carry_a_ref[...]
    prev_b = carry_b_ref[...]
    for c in range(tm // cm):
        r0 = c * cm
        h = h_ref[r0:r0 + cm, :]
        slot = lax.rem(pl.program_id(1) + c, 2)
        ua_ref[slot] = _dot(h, wa_bf_ref[...])
        ub_ref[slot] = _dot(h, wb_bf_ref[...])
        ua = ua_ref[slot]
        ub = ub_ref[slot]
        a_head, a_tail = conv(ua, prev_a, cwa_ref, cba_ref)
        b_head, b_tail = conv(ub, prev_b, cwb_ref, cbb_ref)
        o_ref[r0:r0 + V7X_SUBLANES, :] = (_silu(a_head) * b_head).astype(o_ref.dtype)
        o_ref[r0 + V7X_SUBLANES:r0 + cm, :] = (_silu(a_tail) * b_tail).astype(o_ref.dtype)
        prev_a = ua[cm - V7X_SUBLANES:cm]
        prev_b = ub[cm - V7X_SUBLANES:cm]
    carry_a_ref[...] = prev_a
    carry_b_ref[...] = prev_b


def _slab_rows(rows, steps):
    packed = 2 * V7X_SUBLANES
    return min(m for m in range(packed, rows + 1, packed) if rows % m == 0 and m * steps >= rows)


def _ffn_in(h, w_in, conv_w, conv_b, layer, to_round):
    s, d = h.shape
    f = w_in.shape[-1] // 2
    tm = _tile(s, 1024)
    tn = _tile(f, 256)
    nj = f // tn
    ni = s // tm
    cm = min(FFN_CHUNK_ROWS, tm)
    conv_b = conv_b.reshape(conv_b.shape[0], 1, 2 * f)

    side_in_specs, side_out_specs, side_out_shapes = [], [], []
    for w, w_layer in to_round:
        rows, cols = w.shape[-2:]
        slab = _slab_rows(rows, nj * ni)
        last = rows // slab - 1

        def slab_index(j, i, last=last):
            return jnp.minimum(j * ni + i, last)

        side_in_specs.append(pl.BlockSpec(
            (None, slab, cols), lambda j, i, w_layer=w_layer, f=slab_index: (w_layer, f(j, i), 0)))
        side_out_specs.append(pl.BlockSpec((slab, cols), lambda j, i, f=slab_index: (f(j, i), 0)))
        side_out_shapes.append(jax.ShapeDtypeStruct((rows, cols), BF16))

    outs = pl.pallas_call(
        functools.partial(_ffn_in_kernel, n_side=len(to_round)),
        grid=(nj, ni),
        in_specs=[pl.BlockSpec((tm, d), lambda j, i: (i, 0)),
                  pl.BlockSpec((None, d, tn), lambda j, i: (layer, 0, j)),
                  pl.BlockSpec((None, d, tn), lambda j, i: (layer, 0, j + nj)),
                  pl.BlockSpec((None, CONV_WIDTH, tn), lambda j, i: (layer, 0, j)),
                  pl.BlockSpec((None, CONV_WIDTH, tn), lambda j, i: (layer, 0, j + nj)),
                  pl.BlockSpec((None, 1, tn), lambda j, i: (layer, 0, j)),
                  pl.BlockSpec((None, 1, tn), lambda j, i: (layer, 0, j + nj))] + side_in_specs,
        out_specs=[pl.BlockSpec((tm, tn), lambda j, i: (i, j))] + side_out_specs,
        out_shape=[jax.ShapeDtypeStruct((s, f), BF16)] + side_out_shapes,
        scratch_shapes=[pltpu.VMEM((d, tn), BF16),
                        pltpu.VMEM((d, tn), BF16),
                        pltpu.VMEM((V7X_SUBLANES, tn), F32),
                        pltpu.VMEM((V7X_SUBLANES, tn), F32),
                        pltpu.VMEM((2, cm, tn), F32),
                        pltpu.VMEM((2, cm, tn), F32)],
        compiler_params=_params("arbitrary", "arbitrary"),
        name="ffn_in",
    )(h, w_in, w_in, conv_w, conv_w, conv_b, conv_b, *(w for w, _ in to_round))
    return outs[0], list(outs[1:])


def _gla_kernel(qk_ref, v_ref, r_ref, gl_ref, wgu_ref, bg_ref, ng_ref, tril_ref,
                o_ref, state_ref, *, pairs, dk, dv):
    kw = GLA_HEADS * dk
    pair_rows = 2 * GLA_CHUNK

    @pl.when(pl.program_id(0) == 0)
    def _():
        state_ref[...] = jnp.zeros_like(state_ref)

    tril = tril_ref[...]
    own_causal = tril > 0
    r_id = lax.broadcasted_iota(jnp.int32, (pair_rows, pair_rows), 0)
    c_id = lax.broadcasted_iota(jnp.int32, (pair_rows, pair_rows), 1)
    b_sees_a = jnp.logical_and(r_id >= GLA_CHUNK, c_id < GLA_CHUNK)
    in_b = lax.broadcasted_iota(jnp.int32, (pair_rows, 1), 0) >= GLA_CHUNK
    q_scale = dk ** -0.5

    def pair(p, carry):
        rows = pl.ds(pl.multiple_of(p * pair_rows, pair_rows), pair_rows)
        y = _dot(gl_ref[rows, :].astype(BF16), wgu_ref[...]) + bg_ref[...]
        log_alpha = (jnp.minimum(y, 0.0) - jnp.log(1.0 + jnp.exp(-jnp.abs(y)))) * (1.0 / GLA_TAU)
        hi, lo = _split_bf16(log_alpha)
        b_cum = _dot(tril, hi) + _dot(tril, lo)
        b_last_a = b_cum[GLA_CHUNK - 1:GLA_CHUNK, :]
        b_last_b = b_cum[pair_rows - 1:pair_rows, :]
        d_a = jnp.exp(b_last_a)
        d_b = jnp.exp(b_last_b)
        e_pos = jnp.exp(b_cum)
        e_neg = jnp.exp(-b_cum)
        e_end = jnp.exp(jnp.where(in_b, b_last_b, b_last_a) - b_cum)
        e_pos_s = e_pos * jnp.where(in_b, d_a, 1.0)
        e_end_s = e_end * jnp.where(in_b, 1.0, d_b)
        decay_rows = jnp.broadcast_to(d_a * d_b, (V7X_LANES, kw))
        for h in range(GLA_HEADS):
            ksl = slice(h * dk, (h + 1) * dk)
            vsl = slice(h * dv, (h + 1) * dv)
            q = qk_ref[rows, ksl] * q_scale
            k = qk_ref[rows, kw + h * dk:kw + (h + 1) * dk]
            q_dec = (q * e_pos[:, ksl]).astype(BF16)
            k_dec = (k * e_neg[:, ksl]).astype(BF16)
            k_end = (k * e_end[:, ksl]).astype(BF16)
            q_dec_s = (q * e_pos_s[:, ksl]).astype(BF16)
            k_end_s = (k * e_end_s[:, ksl]).astype(BF16)
            v = v_ref[rows, vsl]
            scores = jnp.where(own_causal, _dot_nt(q_dec, k_dec),
                               jnp.where(b_sees_a, _dot_nt(q_dec, k_end), 0.0)).astype(BF16)
            state = state_ref[h]
            o = _dot(scores, v) + _dot(q_dec_s, state.astype(BF16))
            decay = jnp.concatenate(
                [decay_rows[:, h * dk + b * V7X_LANES:h * dk + (b + 1) * V7X_LANES].T
                 for b in range(dk // V7X_LANES)], axis=0)
            decay = jnp.concatenate([decay] * (dv // V7X_LANES), axis=1)
            state_ref[h] = state * decay + _dot_tn(k_end_s, v)
            ms = jnp.mean(o * o, axis=-1, keepdims=True)
            o = o * lax.rsqrt(ms + EPS) * ng_ref[...]
            o_ref[rows, vsl] = (_silu(r_ref[rows, vsl]) * o).astype(o_ref.dtype)
        return carry

    lax.fori_loop(0, pairs, pair, 0)


def _gla(qk, v, r, g_low, w_gate_up, b_gate, norm_g):
    s = qk.shape[0]
    kw = w_gate_up.shape[1]
    vw = v.shape[1]
    dk, dv = kw // GLA_HEADS, vw // GLA_HEADS
    rows = _tile(s, 256)
    pair_rows = 2 * GLA_CHUNK
    pairs = rows // pair_rows
    t = jnp.arange(pair_rows)
    tril = jnp.logical_and(t[:, None] >= t[None, :],
                           t[:, None] // GLA_CHUNK == t[None, :] // GLA_CHUNK).astype(BF16)
    return pl.pallas_call(
        functools.partial(_gla_kernel, pairs=pairs, dk=dk, dv=dv),
        grid=(s // rows,),
        in_specs=[pl.BlockSpec((rows, 2 * kw), lambda i: (i, 0)),
                  pl.BlockSpec((rows, vw), lambda i: (i, 0)),
                  pl.BlockSpec((rows, vw), lambda i: (i, 0)),
                  pl.BlockSpec((rows, V7X_LANES), lambda i: (i, 0)),
                  pl.BlockSpec((V7X_LANES, kw), lambda i: (0, 0)),
                  pl.BlockSpec((1, kw), lambda i: (0, 0)),
                  pl.BlockSpec((1, dv), lambda i: (0, 0)),
                  pl.BlockSpec((pair_rows, pair_rows), lambda i: (0, 0))],
        out_specs=pl.BlockSpec((rows, vw), lambda i: (i, 0)),
        out_shape=jax.ShapeDtypeStruct((s, vw), BF16),
        scratch_shapes=[pltpu.VMEM((GLA_HEADS, dk, dv), F32)],
        compiler_params=_params("arbitrary"),
        name="gla",
    )(qk, v, r, g_low, w_gate_up, b_gate.reshape(1, kw), norm_g.reshape(1, dv), tril)


def _sb_kernel(q_ref, k_ref, v_ref, t_ref, o_ref, acc_ref, run_ref, *, tq, tk, heads):
    qi = pl.program_id(1)
    tmat = t_ref[...]
    row_groups = tq // tk
    base = qi * row_groups
    below_diag = (lax.broadcasted_iota(jnp.int32, (tk, tk), 1)
                  < lax.broadcasted_iota(jnp.int32, (tk, tk), 0))

    chains = heads * row_groups
    all_chains = range(chains)
    group = [c % row_groups for c in all_chains]
    rows = [slice(g * tk, (g + 1) * tk) for g in group]
    lanes = [slice((c // row_groups) * SB_HEAD_DIM, (c // row_groups + 1) * SB_HEAD_DIM)
             for c in all_chains]

    def sweep(blocks, diagonal, live):
        starts = [pl.multiple_of(blocks[g] * tk, tk) for g in group]
        if live is not None:
            live = [live[g] for g in group]
        zs = [_dot_nt(q_ref[rows[c], lanes[c]], k_ref[pl.ds(starts[c], tk), lanes[c]])
              for c in all_chains]
        log_betas, halves = [], []
        for z in zs:
            softplus = jnp.maximum(z, 0.0) + jnp.log(1.0 + jnp.exp(-jnp.abs(z)))
            log_betas.append(z - softplus)
            neg_log_rest = softplus
            if diagonal:
                neg_log_rest = jnp.where(below_diag, neg_log_rest, 0.0)
            halves.append(jnp.concatenate(_split_bf16(neg_log_rest), axis=1))
        cms = [_dot(hl, tmat) for hl in halves]
        atts, runs = [], []
        for c in all_chains:
            log_att = log_betas[c] + cms[c][:, :tk]
            if diagonal:
                atts.append(jnp.where(below_diag, jnp.exp(log_att), 0.0))
                runs.append(cms[c][:, tk:])
            else:
                run = jnp.where(live[c], run_ref[rows[c], lanes[c]], -1e30)
                atts.append(jnp.exp(log_att + run))
                runs.append(run + cms[c][:, tk:])
        pvs = [_dot(atts[c].astype(BF16), v_ref[pl.ds(starts[c], tk), lanes[c]])
               for c in all_chains]
        if not diagonal:
            pvs = [acc_ref[rows[c], lanes[c]] + pvs[c] for c in all_chains]
        for c in all_chains:
            acc_ref[rows[c], lanes[c]] = pvs[c]
            run_ref[rows[c], lanes[c]] = runs[c]
        if diagonal:
            return jnp.float32(0.0)
        return jnp.max(functools.reduce(jnp.maximum, runs))

    sweep([base + g for g in range(row_groups)], True, None)

    def cond(st):
        n, top = st
        return jnp.logical_and(n <= base + row_groups - 1, top > F32_EXP_UNDERFLOW)

    def body(st):
        n, _ = st
        blocks = [base + g - n for g in range(row_groups)]
        top = sweep([jnp.maximum(j, 0) for j in blocks], False, [j >= 0 for j in blocks])
        return n + 1, top

    lax.while_loop(cond, body, body((jnp.int32(1), jnp.float32(0.0))))
    o_ref[...] = acc_ref[...].astype(o_ref.dtype)


def _sb_attention(q, k, v):
    s, d = q.shape
    dh = SB_HEAD_DIM
    tk = V7X_LANES
    tq = _tile(s, 1024)
    r = lax.broadcasted_iota(jnp.int32, (tk, 2 * tk), 0)
    c = lax.broadcasted_iota(jnp.int32, (tk, 2 * tk), 1)
    half = jnp.where(jnp.logical_or(c >= tk, r > c), -1.0, 0.0).astype(BF16)
    tmat = jnp.concatenate([half, half], axis=0)
    heads = SB_HEADS_PER_STEP
    wh = heads * dh
    return pl.pallas_call(
        functools.partial(_sb_kernel, tq=tq, tk=tk, heads=heads),
        grid=(d // wh, s // tq),
        in_specs=[pl.BlockSpec((tq, wh), lambda h, i: (i, h)),
                  pl.BlockSpec((s, wh), lambda h, i: (0, h)),
                  pl.BlockSpec((s, wh), lambda h, i: (0, h)),
                  pl.BlockSpec((2 * tk, 2 * tk), lambda h, i: (0, 0))],
        out_specs=pl.BlockSpec((tq, wh), lambda h, i: (i, h)),
        out_shape=jax.ShapeDtypeStruct((s, d), BF16),
        scratch_shapes=[pltpu.VMEM((tq, wh), F32), pltpu.VMEM((tq, wh), F32)],
        compiler_params=_params("arbitrary", "arbitrary"),
        name="sb_attention",
    )(q, k, v, tmat)


def kernel(x, c, w_ada, b_ada, ada_table, gla_w_in, gla_w_gate_up, gla_b_gate, gla_norm_g, gla_w_out, kv_norm_g, w_kv, k_norm_g, sb_w_q, sb_q_norm_g, sb_w_out, ffn_w_in, ffn_conv_w, ffn_conv_b, ffn_w_out):
    batch, s, d = x.shape
    assert batch == 1, "adaLN modulation rows are built for a single sequence"
    depth = ada_table.shape[0]
    n_a = gla_w_in.shape[0]
    kw = gla_w_gate_up.shape[2]
    rank = gla_w_gate_up.shape[1]
    x = x.reshape(s, d)

    w_up_bf = jnp.pad(gla_w_gate_up, ((0, 0), (0, V7X_LANES - rank), (0, 0))).astype(BF16)
    w_kv3 = w_kv.reshape(1, d, 2 * d)

    def mixer_weights(l):
        if l >= depth:
            return []
        if l < n_a:
            return [(gla_w_out, l)]
        own = [(sb_w_q, l - n_a), (sb_w_out, l - n_a)]
        return ([(w_kv3, 0)] if l == n_a else []) + own

    gla_w_in_bf = gla_w_in.astype(BF16)
    mixer_bf = [w[i].astype(BF16) for w, i in mixer_weights(0)]

    mods = _cond(c, w_ada, b_ada, ada_table)
    k_sh = v_sh = None
    for l in range(depth):
        mod = mods[l]
        shift, scale, gate = (mod[i * d:(i + 1) * d] for i in (0, 1, 2))
        affines = [(scale, shift, True)]
        if l == n_a:
            affines.append((kv_norm_g, jnp.zeros_like(kv_norm_g), False))
        h, *kv_src = _norm_affine(x, affines)
        if l < n_a:
            (w_out_bf,) = mixer_bf
            qk = _mm_plain(h, gla_w_in_bf, F32, layer=l, col0=0, n=2 * kw)
            v = _mm_plain(h, gla_w_in_bf, BF16, layer=l, col0=2 * kw, n=d)
            r = _mm_plain(h, gla_w_in_bf, F32, layer=l, col0=2 * kw + d, n=d)
            g_low = _mm_plain(h, gla_w_in_bf, F32, layer=l, col0=2 * kw + 2 * d,
                              n=V7X_LANES, valid_cols=rank)
            o = _gla(qk, v, r, g_low, w_up_bf[l], gla_b_gate[l], gla_norm_g[l])
        else:
            if l == n_a:
                w_kv_bf = mixer_bf.pop(0)
                (src,) = kv_src
                k_sh = _mm_headnorm(src, w_kv_bf, k_norm_g, 1.0, col0=0, n=d)
                v_sh = _mm_plain(src, w_kv_bf, BF16, col0=d, n=d)
            w_q_bf, w_out_bf = mixer_bf
            q = _mm_headnorm(h, w_q_bf, sb_q_norm_g[l - n_a], SB_HEAD_DIM ** -0.5)
            o = _sb_attention(q, k_sh, v_sh)
        x = _mm_residual(o, w_out_bf, x, gate)

        shift, scale, gate = (mod[i * d:(i + 1) * d] for i in (3, 4, 5))
        (h,) = _norm_affine(x, [(scale, shift, True)])
        g, rounded = _ffn_in(h, ffn_w_in, ffn_conv_w, ffn_conv_b, l,
                             [(ffn_w_out, l)] + mixer_weights(l + 1))
        x = _mm_residual(g, rounded[0], x, gate)
        mixer_bf = rounded[1:]
    return x.reshape(batch, s, d)
```

```python
import functools

import jax
import jax.numpy as jnp
from jax import lax
from jax.experimental import pallas as pl
from jax.experimental.pallas import tpu as pltpu

F32 = jnp.float32
BF16 = jnp.bfloat16

EPS = 1e-6
CONV_WIDTH = 3
GLA_CHUNK = 64
GLA_HEADS = 4
GLA_TAU = 16.0
SB_HEAD_DIM = 128
SB_HEADS_PER_STEP = 2

V7X_LANES = 128
V7X_SUBLANES = 8
V7X_VMEM_LIMIT_BYTES = 56 * 1024 * 1024

F32_EXP_UNDERFLOW = -88.0


def _params(*semantics):
    return pltpu.CompilerParams(dimension_semantics=semantics,
                                vmem_limit_bytes=V7X_VMEM_LIMIT_BYTES)


def _tile(n, pref):
    if n <= pref:
        return n
    t = (pref // V7X_LANES) * V7X_LANES
    while t >= V7X_LANES:
        if n % t == 0:
            return t
        t -= V7X_LANES
    return n


def _dot(a, b):
    return jnp.dot(a, b, preferred_element_type=F32)


def _dot_nt(a, b):
    return lax.dot_general(a, b, (((1,), (1,)), ((), ())), preferred_element_type=F32)


def _dot_tn(a, b):
    return lax.dot_general(a, b, (((0,), (0,)), ((), ())), preferred_element_type=F32)


def _silu(x):
    return x / (1.0 + jnp.exp(-x))


def _split_bf16(x):
    hi = x.astype(BF16)
    lo = (x - hi.astype(F32)).astype(BF16)
    return hi, lo


def _cond_kernel(c_ref, w_ref, b_ref, tab_ref, o_ref):
    s = _silu(c_ref[...]).astype(BF16)
    acc = _dot(s, w_ref[...].astype(BF16))
    o_ref[...] = acc[0:1, :] + b_ref[...] + tab_ref[...]


def _cond(c, w_ada, b_ada, ada_table):
    d = c.shape[1]
    depth = ada_table.shape[0]
    n = w_ada.shape[1]
    tn = _tile(n, 512)
    c8 = jnp.broadcast_to(c, (V7X_SUBLANES, d))
    return pl.pallas_call(
        _cond_kernel,
        grid=(n // tn,),
        in_specs=[pl.BlockSpec((V7X_SUBLANES, d), lambda j: (0, 0)),
                  pl.BlockSpec((d, tn), lambda j: (0, j)),
                  pl.BlockSpec((1, tn), lambda j: (0, j)),
                  pl.BlockSpec((depth, tn), lambda j: (0, j))],
        out_specs=pl.BlockSpec((depth, tn), lambda j: (0, j)),
        out_shape=jax.ShapeDtypeStruct((depth, n), F32),
        compiler_params=_params("arbitrary"),
        name="cond",
    )(c8, w_ada, b_ada.reshape(1, n), ada_table.reshape(depth, n))


def _norm_affine_kernel(x_ref, *refs, plus_one):
    n = len(plus_one)
    x = x_ref[...]
    ms = jnp.mean(x * x, axis=-1, keepdims=True)
    xn = x * lax.rsqrt(ms + EPS)
    for t in range(n):
        mul_ref, add_ref, o_ref = refs[2 * t], refs[2 * t + 1], refs[2 * n + t]
        mul = mul_ref[...]
        if plus_one[t]:
            mul = 1.0 + mul
        o_ref[...] = (xn * mul + add_ref[...]).astype(o_ref.dtype)


def _norm_affine(x, affines):
    s, d = x.shape
    tm = _tile(s, 512 if len(affines) == 1 else 256)
    vec = pl.BlockSpec((1, d), lambda i: (0, 0))
    row = pl.BlockSpec((tm, d), lambda i: (i, 0))
    operands = [v.reshape(1, d) for mul, add, _ in affines for v in (mul, add)]
    return pl.pallas_call(
        functools.partial(_norm_affine_kernel, plus_one=tuple(p for _, _, p in affines)),
        grid=(s // tm,),
        in_specs=[row] + [vec] * len(operands),
        out_specs=[row] * len(affines),
        out_shape=[jax.ShapeDtypeStruct((s, d), BF16)] * len(affines),
        compiler_params=_params("arbitrary"),
        name="norm_affine",
    )(x, *operands)


def _mm_plain_kernel(a_ref, w_ref, o_ref, *, valid_cols=None):
    w = w_ref[...]
    if valid_cols is not None:
        col = lax.broadcasted_iota(jnp.int32, w.shape, 1)
        w = jnp.where(col < valid_cols, w, jnp.zeros_like(w))
    o_ref[...] = _dot(a_ref[...], w).astype(o_ref.dtype)


def _mm_headnorm_kernel(a_ref, w_ref, g_ref, o_ref, *, post_scale):
    acc = _dot(a_ref[...], w_ref[...])
    g = g_ref[...] * post_scale
    for c in range(acc.shape[1] // SB_HEAD_DIM):
        sl = slice(c * SB_HEAD_DIM, (c + 1) * SB_HEAD_DIM)
        blk = acc[:, sl]
        ms = jnp.mean(blk * blk, axis=-1, keepdims=True)
        o_ref[:, sl] = (blk * lax.rsqrt(ms + EPS) * g).astype(o_ref.dtype)


def _mm_residual_kernel(a_ref, w_ref, x_ref, gate_ref, o_ref):
    o_ref[...] = x_ref[...] + gate_ref[...] * _dot(a_ref[...], w_ref[...])


def _mm_tiles(m, n, k):
    tm = _tile(m, 1024)
    tn = _tile(n, 1024 if k <= 4096 else 512)
    if k > 4096:
        tm = _tile(m, 512)
    return tm, tn


def _weight_spec(w, layer, col0, tn):
    k = w.shape[-2]
    assert col0 % tn == 0
    c0 = col0 // tn
    if w.ndim == 2:
        return pl.BlockSpec((k, tn), lambda i, j: (0, c0 + j))
    return pl.BlockSpec((None, k, tn), lambda i, j: (layer, 0, c0 + j))


def _mm_plain(a, w, out_dtype, *, layer=0, col0=0, n=None, valid_cols=None):
    m, k = a.shape
    n = w.shape[-1] if n is None else n
    tm, tn = _mm_tiles(m, n, k)
    return pl.pallas_call(
        functools.partial(_mm_plain_kernel, valid_cols=valid_cols),
        grid=(m // tm, n // tn),
        in_specs=[pl.BlockSpec((tm, k), lambda i, j: (i, 0)),
                  _weight_spec(w, layer, col0, tn)],
        out_specs=pl.BlockSpec((tm, tn), lambda i, j: (i, j)),
        out_shape=jax.ShapeDtypeStruct((m, n), out_dtype),
        compiler_params=_params("arbitrary", "arbitrary"),
        name="mm_plain",
    )(a, w)


def _mm_headnorm(a, w, g, post_scale, *, col0=0, n=None):
    m, k = a.shape
    n = w.shape[-1] if n is None else n
    tm, tn = _mm_tiles(m, n, k)
    return pl.pallas_call(
        functools.partial(_mm_headnorm_kernel, post_scale=post_scale),
        grid=(m // tm, n // tn),
        in_specs=[pl.BlockSpec((tm, k), lambda i, j: (i, 0)),
                  _weight_spec(w, 0, col0, tn),
                  pl.BlockSpec((1, SB_HEAD_DIM), lambda i, j: (0, 0))],
        out_specs=pl.BlockSpec((tm, tn), lambda i, j: (i, j)),
        out_shape=jax.ShapeDtypeStruct((m, n), BF16),
        compiler_params=_params("arbitrary", "arbitrary"),
        name="mm_headnorm",
    )(a, w, g.reshape(1, SB_HEAD_DIM))


def _mm_residual(a, w, x, gate):
    m, k = a.shape
    n = w.shape[-1]
    tm, tn = _mm_tiles(m, n, k)
    return pl.pallas_call(
        _mm_residual_kernel,
        grid=(m // tm, n // tn),
        in_specs=[pl.BlockSpec((tm, k), lambda i, j: (i, 0)),
                  _weight_spec(w, 0, 0, tn),
                  pl.BlockSpec((tm, tn), lambda i, j: (i, j)),
                  pl.BlockSpec((1, tn), lambda i, j: (0, j))],
        out_specs=pl.BlockSpec((tm, tn), lambda i, j: (i, j)),
        out_shape=jax.ShapeDtypeStruct((m, n), F32),
        compiler_params=_params("arbitrary", "arbitrary"),
        name="mm_residual",
    )(a, w, x, gate.reshape(1, n))


FFN_CHUNK_ROWS = 128


def _ffn_in_kernel(h_ref, wa_ref, wb_ref, cwa_ref, cwb_ref, cba_ref, cbb_ref, *refs, n_side):
    side_f32_refs = refs[:n_side]
    o_ref = refs[n_side]
    side_bf_refs = refs[n_side + 1:2 * n_side + 1]
    wa_bf_ref, wb_bf_ref, carry_a_ref, carry_b_ref, ua_ref, ub_ref = refs[2 * n_side + 1:]
    tm = h_ref.shape[0]
    cm = min(FFN_CHUNK_ROWS, tm)

    for src_ref, dst_ref in zip(side_f32_refs, side_bf_refs):
        dst_ref[...] = src_ref[...].astype(BF16)

    @pl.when(pl.program_id(1) == 0)
    def _():
        wa_bf_ref[...] = wa_ref[...].astype(BF16)
        wb_bf_ref[...] = wb_ref[...].astype(BF16)
        carry_a_ref[...] = jnp.zeros_like(carry_a_ref)
        carry_b_ref[...] = jnp.zeros_like(carry_b_ref)

    row = lax.broadcasted_iota(jnp.int32, carry_a_ref.shape, 0)

    def conv(u, prev, cw_ref, cb_ref):
        w0, w1, w2 = cw_ref[0:1, :], cw_ref[1:2, :], cw_ref[2:3, :]
        bias = cb_ref[...]
        u1 = pltpu.roll(u, 1, axis=0)
        u2 = pltpu.roll(u, 2, axis=0)
        head1 = jnp.where(row < 1, pltpu.roll(prev, 1, axis=0), u1[0:V7X_SUBLANES])
        head2 = jnp.where(row < 2, pltpu.roll(prev, 2, axis=0), u2[0:V7X_SUBLANES])
        head = bias + head2 * w0 + head1 * w1 + u[0:V7X_SUBLANES] * w2
        tail = (bias + u2[V7X_SUBLANES:] * w0 + u1[V7X_SUBLANES:] * w1
                + u[V7X_SUBLANES:] * w2)
        return head, tail

    prev_a = carry_a_ref[...]
    prev_b = carry_b_ref[...]
    for c in range(tm // cm):
        r0 = c * cm
        h = h_ref[r0:r0 + cm, :]
        slot = lax.rem(pl.program_id(1) + c, 2)
        ua_ref[slot] = _dot(h, wa_bf_ref[...])
        ub_ref[slot] = _dot(h, wb_bf_ref[...])
        ua = ua_ref[slot]
        ub = ub_ref[slot]
        a_head, a_tail = conv(ua, prev_a, cwa_ref, cba_ref)
        b_head, b_tail = conv(ub, prev_b, cwb_ref, cbb_ref)
        o_ref[r0:r0 + V7X_SUBLANES, :] = (_silu(a_head) * b_head).astype(o_ref.dtype)
        o_ref[r0 + V7X_SUBLANES:r0 + cm, :] = (_silu(a_tail) * b_tail).astype(o_ref.dtype)
        prev_a = ua[cm - V7X_SUBLANES:cm]
        prev_b = ub[cm - V7X_SUBLANES:cm]
    carry_a_ref[...] = prev_a
    carry_b_ref[...] = prev_b


def _slab_rows(rows, steps):
    packed = 2 * V7X_SUBLANES
    return min(m for m in range(packed, rows + 1, packed) if rows % m == 0 and m * steps >= rows)


def _ffn_in(h, w_in, conv_w, conv_b, layer, to_round):
    s, d = h.shape
    f = w_in.shape[-1] // 2
    tm = _tile(s, 1024)
    tn = _tile(f, 256)
    nj = f // tn
    ni = s // tm
    cm = min(FFN_CHUNK_ROWS, tm)
    conv_b = conv_b.reshape(conv_b.shape[0], 1, 2 * f)

    side_in_specs, side_out_specs, side_out_shapes = [], [], []
    for w, w_layer in to_round:
        rows, cols = w.shape[-2:]
        slab = _slab_rows(rows, nj * ni)
        last = rows // slab - 1

        def slab_index(j, i, last=last):
            return jnp.minimum(j * ni + i, last)

        side_in_specs.append(pl.BlockSpec(
            (None, slab, cols), lambda j, i, w_layer=w_layer, f=slab_index: (w_layer, f(j, i), 0)))
        side_out_specs.append(pl.BlockSpec((slab, cols), lambda j, i, f=slab_index: (f(j, i), 0)))
        side_out_shapes.append(jax.ShapeDtypeStruct((rows, cols), BF16))

    outs = pl.pallas_call(
        functools.partial(_ffn_in_kernel, n_side=len(to_round)),
        grid=(nj, ni),
        in_specs=[pl.BlockSpec((tm, d), lambda j, i: (i, 0)),
                  pl.BlockSpec((None, d, tn), lambda j, i: (layer, 0, j)),
                  pl.BlockSpec((None, d, tn), lambda j, i: (layer, 0, j + nj)),
                  pl.BlockSpec((None, CONV_WIDTH, tn), lambda j, i: (layer, 0, j)),
                  pl.BlockSpec((None, CONV_WIDTH, tn), lambda j, i: (layer, 0, j + nj)),
                  pl.BlockSpec((None, 1, tn), lambda j, i: (layer, 0, j)),
                  pl.BlockSpec((None, 1, tn), lambda j, i: (layer, 0, j + nj))] + side_in_specs,
        out_specs=[pl.BlockSpec((tm, tn), lambda j, i: (i, j))] + side_out_specs,
        out_shape=[jax.ShapeDtypeStruct((s, f), BF16)] + side_out_shapes,
        scratch_shapes=[pltpu.VMEM((d, tn), BF16),
                        pltpu.VMEM((d, tn), BF16),
                        pltpu.VMEM((V7X_SUBLANES, tn), F32),
                        pltpu.VMEM((V7X_SUBLANES, tn), F32),
                        pltpu.VMEM((2, cm, tn), F32),
                        pltpu.VMEM((2, cm, tn), F32)],
        compiler_params=_params("arbitrary", "arbitrary"),
        name="ffn_in",
    )(h, w_in, w_in, conv_w, conv_w, conv_b, conv_b, *(w for w, _ in to_round))
    return outs[0], list(outs[1:])


def _gla_kernel(qk_ref, v_ref, r_ref, gl_ref, wgu_ref, bg_ref, ng_ref, tril_ref,
                o_ref, state_ref, *, pairs, dk, dv):
    kw = GLA_HEADS * dk
    pair_rows = 2 * GLA_CHUNK

    @pl.when(pl.program_id(0) == 0)
    def _():
        state_ref[...] = jnp.zeros_like(state_ref)

    tril = tril_ref[...]
    own_causal = tril > 0
    r_id = lax.broadcasted_iota(jnp.int32, (pair_rows, pair_rows), 0)
    c_id = lax.broadcasted_iota(jnp.int32, (pair_rows, pair_rows), 1)
    b_sees_a = jnp.logical_and(r_id >= GLA_CHUNK, c_id < GLA_CHUNK)
    in_b = lax.broadcasted_iota(jnp.int32, (pair_rows, 1), 0) >= GLA_CHUNK
    q_scale = dk ** -0.5

    def pair(p, carry):
        rows = pl.ds(pl.multiple_of(p * pair_rows, pair_rows), pair_rows)
        y = _dot(gl_ref[rows, :].astype(BF16), wgu_ref[...]) + bg_ref[...]
        log_alpha = (jnp.minimum(y, 0.0) - jnp.log(1.0 + jnp.exp(-jnp.abs(y)))) * (1.0 / GLA_TAU)
        hi, lo = _split_bf16(log_alpha)
        b_cum = _dot(tril, hi) + _dot(tril, lo)
        b_last_a = b_cum[GLA_CHUNK - 1:GLA_CHUNK, :]
        b_last_b = b_cum[pair_rows - 1:pair_rows, :]
        d_a = jnp.exp(b_last_a)
        d_b = jnp.exp(b_last_b)
        decay_rows = jnp.broadcast_to(d_a * d_b, (V7X_LANES, kw))
        for h in range(GLA_HEADS):
            ksl = slice(h * dk, (h + 1) * dk)
            vsl = slice(h * dv, (h + 1) * dv)
            b_h = b_cum[:, ksl]
            e_pos = jnp.exp(b_h)
            e_neg = jnp.exp(-b_h)
            e_end = jnp.exp(jnp.where(in_b, b_last_b[:, ksl], b_last_a[:, ksl]) - b_h)
            e_pos_s = e_pos * jnp.where(in_b, d_a[:, ksl], 1.0)
            e_end_s = e_end * jnp.where(in_b, 1.0, d_b[:, ksl])
            q = qk_ref[rows, ksl] * q_scale
            k = qk_ref[rows, kw + h * dk:kw + (h + 1) * dk]
            q_dec = (q * e_pos).astype(BF16)
            k_dec = (k * e_neg).astype(BF16)
            k_end = (k * e_end).astype(BF16)
            q_dec_s = (q * e_pos_s).astype(BF16)
            k_end_s = (k * e_end_s).astype(BF16)
            v = v_ref[rows, vsl]
            scores = jnp.where(own_causal, _dot_nt(q_dec, k_dec),
                               jnp.where(b_sees_a, _dot_nt(q_dec, k_end), 0.0)).astype(BF16)
            state = state_ref[h]
            o = _dot(scores, v) + _dot(q_dec_s, state.astype(BF16))
            decay = jnp.concatenate(
                [decay_rows[:, h * dk + b * V7X_LANES:h * dk + (b + 1) * V7X_LANES].T
                 for b in range(dk // V7X_LANES)], axis=0)
            decay = jnp.concatenate([decay] * (dv // V7X_LANES), axis=1)
            state_ref[h] = state * decay + _dot_tn(k_end_s, v)
            ms = jnp.mean(o * o, axis=-1, keepdims=True)
            o = o * lax.rsqrt(ms + EPS) * ng_ref[...]
            o_ref[rows, vsl] = (_silu(r_ref[rows, vsl]) * o).astype(o_ref.dtype)
        return carry

    lax.fori_loop(0, pairs, pair, 0)


def _gla(qk, v, r, g_low, w_gate_up, b_gate, norm_g):
    s = qk.shape[0]
    kw = w_gate_up.shape[1]
    vw = v.shape[1]
    dk, dv = kw // GLA_HEADS, vw // GLA_HEADS
    rows = _tile(s, 256)
    pair_rows = 2 * GLA_CHUNK
    pairs = rows // pair_rows
    t = jnp.arange(pair_rows)
    tril = jnp.logical_and(t[:, None] >= t[None, :],
                           t[:, None] // GLA_CHUNK == t[None, :] // GLA_CHUNK).astype(BF16)
    return pl.pallas_call(
        functools.partial(_gla_kernel, pairs=pairs, dk=dk, dv=dv),
        grid=(s // rows,),
        in_specs=[pl.BlockSpec((rows, 2 * kw), lambda i: (i, 0)),
                  pl.BlockSpec((rows, vw), lambda i: (i, 0)),
                  pl.BlockSpec((rows, vw), lambda i: (i, 0)),
                  pl.BlockSpec((rows, V7X_LANES), lambda i: (i, 0)),
                  pl.BlockSpec((V7X_LANES, kw), lambda i: (0, 0)),
                  pl.BlockSpec((1, kw), lambda i: (0, 0)),
                  pl.BlockSpec((1, dv), lambda i: (0, 0)),
                  pl.BlockSpec((pair_rows, pair_rows), lambda i: (0, 0))],
        out_specs=pl.BlockSpec((rows, vw), lambda i: (i, 0)),
        out_shape=jax.ShapeDtypeStruct((s, vw), BF16),
        scratch_shapes=[pltpu.VMEM((GLA_HEADS, dk, dv), F32)],
        compiler_params=_params("arbitrary"),
        name="gla",
    )(qk, v, r, g_low, w_gate_up, b_gate.reshape(1, kw), norm_g.reshape(1, dv), tril)


def _sb_kernel(q_ref, k_ref, v_ref, t_ref, o_ref, acc_ref, run_ref, *, tq, tk, heads):
    qi = pl.program_id(1)
    tmat = t_ref[...]
    row_groups = tq // tk
    base = qi * row_groups
    below_diag = (lax.broadcasted_iota(jnp.int32, (tk, tk), 1)
                  < lax.broadcasted_iota(jnp.int32, (tk, tk), 0))

    chains = heads * row_groups
    all_chains = range(chains)
    group = [c % row_groups for c in all_chains]
    rows = [slice(g * tk, (g + 1) * tk) for g in group]
    lanes = [slice((c // row_groups) * SB_HEAD_DIM, (c // row_groups + 1) * SB_HEAD_DIM)
             for c in all_chains]

    def sweep(blocks, diagonal, live, which):
        starts = [pl.multiple_of(blocks[group[c]] * tk, tk) for c in which]
        zs = [_dot_nt(q_ref[rows[c], lanes[c]], k_ref[pl.ds(st, tk), lanes[c]])
              for c, st in zip(which, starts)]
        log_betas, halves = [], []
        for z in zs:
            softplus = jnp.maximum(z, 0.0) + jnp.log(1.0 + jnp.exp(-jnp.abs(z)))
            log_betas.append(z - softplus)
            neg_log_rest = softplus
            if diagonal:
                neg_log_rest = jnp.where(below_diag, neg_log_rest, 0.0)
            halves.append(jnp.concatenate(_split_bf16(neg_log_rest), axis=1))
        cms = [_dot(hl, tmat) for hl in halves]
        atts, runs = [], []
        for i, c in enumerate(which):
            log_att = log_betas[i] + cms[i][:, :tk]
            if diagonal:
                atts.append(jnp.where(below_diag, jnp.exp(log_att), 0.0))
                runs.append(cms[i][:, tk:])
            else:
                run = jnp.where(live[group[c]], run_ref[rows[c], lanes[c]], -1e30)
                atts.append(jnp.exp(log_att + run))
                runs.append(run + cms[i][:, tk:])
        pvs = [_dot(atts[i].astype(BF16), v_ref[pl.ds(starts[i], tk), lanes[c]])
               for i, c in enumerate(which)]
        if not diagonal:
            pvs = [acc_ref[rows[c], lanes[c]] + pvs[i] for i, c in enumerate(which)]
        for i, c in enumerate(which):
            acc_ref[rows[c], lanes[c]] = pvs[i]
            run_ref[rows[c], lanes[c]] = runs[i]
        tops = {}
        for i, c in enumerate(which):
            hd = c // row_groups
            tops[hd] = runs[i] if hd not in tops else jnp.maximum(tops[hd], runs[i])
        return {hd: jnp.max(r) for hd, r in tops.items()}

    def full_sweep(n, which):
        blocks = [base + g - n for g in range(row_groups)]
        return sweep([jnp.maximum(j, 0) for j in blocks], False, [j >= 0 for j in blocks], which)

    sweep([base + g for g in range(row_groups)], True, None, list(all_chains))
    tops = full_sweep(jnp.int32(1), list(all_chains))

    for hd in range(heads):
        mine = [c for c in all_chains if c // row_groups == hd]

        def cond(st):
            n, top = st
            return jnp.logical_and(n <= base + row_groups - 1, top > F32_EXP_UNDERFLOW)

        def body(st, mine=mine, hd=hd):
            n, _ = st
            return n + 1, full_sweep(n, mine)[hd]

        lax.while_loop(cond, body, (jnp.int32(2), tops[hd]))
    o_ref[...] = acc_ref[...].astype(o_ref.dtype)


def _sb_attention(q, k, v):
    s, d = q.shape
    dh = SB_HEAD_DIM
    tk = V7X_LANES
    tq = _tile(s, 1024)
    r = lax.broadcasted_iota(jnp.int32, (tk, 2 * tk), 0)
    c = lax.broadcasted_iota(jnp.int32, (tk, 2 * tk), 1)
    half = jnp.where(jnp.logical_or(c >= tk, r > c), -1.0, 0.0).astype(BF16)
    tmat = jnp.concatenate([half, half], axis=0)
    heads = SB_HEADS_PER_STEP
    wh = heads * dh
    return pl.pallas_call(
        functools.partial(_sb_kernel, tq=tq, tk=tk, heads=heads),
        grid=(d // wh, s // tq),
        in_specs=[pl.BlockSpec((tq, wh), lambda h, i: (i, h)),
                  pl.BlockSpec((s, wh), lambda h, i: (0, h)),
                  pl.BlockSpec((s, wh), lambda h, i: (0, h)),
                  pl.BlockSpec((2 * tk, 2 * tk), lambda h, i: (0, 0))],
        out_specs=pl.BlockSpec((tq, wh), lambda h, i: (i, h)),
        out_shape=jax.ShapeDtypeStruct((s, d), BF16),
        scratch_shapes=[pltpu.VMEM((tq, wh), F32), pltpu.VMEM((tq, wh), F32)],
        compiler_params=_params("arbitrary", "arbitrary"),
        name="sb_attention",
    )(q, k, v, tmat)


def kernel(x, c, w_ada, b_ada, ada_table, gla_w_in, gla_w_gate_up, gla_b_gate, gla_norm_g, gla_w_out, kv_norm_g, w_kv, k_norm_g, sb_w_q, sb_q_norm_g, sb_w_out, ffn_w_in, ffn_conv_w, ffn_conv_b, ffn_w_out):
    batch, s, d = x.shape
    assert batch == 1, "adaLN modulation rows are built for a single sequence"
    depth = ada_table.shape[0]
    n_a = gla_w_in.shape[0]
    kw = gla_w_gate_up.shape[2]
    rank = gla_w_gate_up.shape[1]
    x = x.reshape(s, d)

    w_up_bf = jnp.pad(gla_w_gate_up, ((0, 0), (0, V7X_LANES - rank), (0, 0))).astype(BF16)
    w_kv3 = w_kv.reshape(1, d, 2 * d)

    def mixer_weights(l):
        if l >= depth:
            return []
        if l < n_a:
            return [(gla_w_out, l)]
        own = [(sb_w_q, l - n_a), (sb_w_out, l - n_a)]
        return ([(w_kv3, 0)] if l == n_a else []) + own

    gla_w_in_bf = gla_w_in.astype(BF16)
    mixer_bf = [w[i].astype(BF16) for w, i in mixer_weights(0)]

    mods = _cond(c, w_ada, b_ada, ada_table)
    k_sh = v_sh = None
    for l in range(depth):
        mod = mods[l]
        shift, scale, gate = (mod[i * d:(i + 1) * d] for i in (0, 1, 2))
        affines = [(scale, shift, True)]
        if l == n_a:
            affines.append((kv_norm_g, jnp.zeros_like(kv_norm_g), False))
        h, *kv_src = _norm_affine(x, affines)
        if l < n_a:
            (w_out_bf,) = mixer_bf
            qk = _mm_plain(h, gla_w_in_bf, F32, layer=l, col0=0, n=2 * kw)
            v = _mm_plain(h, gla_w_in_bf, BF16, layer=l, col0=2 * kw, n=d)
            r = _mm_plain(h, gla_w_in_bf, F32, layer=l, col0=2 * kw + d, n=d)
            g_low = _mm_plain(h, gla_w_in_bf, F32, layer=l, col0=2 * kw + 2 * d,
                              n=V7X_LANES, valid_cols=rank)
            o = _gla(qk, v, r, g_low, w_up_bf[l], gla_b_gate[l], gla_norm_g[l])
        else:
            if l == n_a:
                w_kv_bf = mixer_bf.pop(0)
                (src,) = kv_src
                k_sh = _mm_headnorm(src, w_kv_bf, k_norm_g, 1.0, col0=0, n=d)
                v_sh = _mm_plain(src, w_kv_bf, BF16, col0=d, n=d)
            w_q_bf, w_out_bf = mixer_bf
            q = _mm_headnorm(h, w_q_bf, sb_q_norm_g[l - n_a], SB_HEAD_DIM ** -0.5)
            o = _sb_attention(q, k_sh, v_sh)
        x = _mm_residual(o, w_out_bf, x, gate)

        shift, scale, gate = (mod[i * d:(i + 1) * d] for i in (3, 4, 5))
        (h,) = _norm_affine(x, [(scale, shift, True)])
        g, rounded = _ffn_in(h, ffn_w_in, ffn_conv_w, ffn_conv_b, l,
                             [(ffn_w_out, l)] + mixer_weights(l + 1))
        x = _mm_residual(g, rounded[0], x, gate)
        mixer_bf = rounded[1:]
    return x.reshape(batch, s, d)
```
